```python
import jax, jax.numpy as jnp
from jax import lax
import numpy as np

D_MODEL = 1024
BATCH = 8
SEQ = 4096
DEPTH = 2

HEAD_DIM = 128
N_MEM_TOK = 256
MEM_HEADS = 4
MEM_W = MEM_HEADS * HEAD_DIM
A_GROUPS = ((128, 1), (512, 4), (2048, 16))
N_A_GROUPS = len(A_GROUPS)
A_HEADS = 8
A_W = A_HEADS * HEAD_DIM
B_Q_HEADS = 8
B_KV_HEADS = 2
B_GROUP = B_Q_HEADS // B_KV_HEADS
B_W = B_Q_HEADS * HEAD_DIM
B_KV_W = B_KV_HEADS * HEAD_DIM
MIX_W = A_W
BRANCH_W = MIX_W + MEM_W
IN_A = 3 * N_A_GROUPS * A_W + MEM_W + BRANCH_W
IN_B = B_W + 2 * B_KV_W + MEM_W + BRANCH_W
ROPE_THETA = 500000.0
ROT_DIM_A = HEAD_DIM // 4
AXIAL_THETA = 10000.0
GRID_W = 64
Q_BLOCK = 128
EPS = 1e-6
N_A = (DEPTH + 1) // 2
N_B = DEPTH // 2

kernel_name = 'hybrid_dilated_axial_gqa_encoder'


def rmsnorm(x, g):
    xf = x.astype(jnp.float32)
    y = xf * lax.rsqrt(jnp.mean(xf * xf, axis=-1, keepdims=True) + EPS)
    return (y * g.astype(jnp.float32)).astype(x.dtype)


def rope_angles(pos, dim, theta):
    inv = theta ** (-jnp.arange(0, dim, 2, dtype=jnp.float32) / dim)
    return pos.astype(jnp.float32)[:, None] * inv[None, :]


def apply_rope(x, ang):
    half = ang.shape[-1]
    rd = 2 * half
    shape = (1, ang.shape[0]) + (1,) * (x.ndim - 3) + (half,)
    cos = jnp.cos(ang).reshape(shape)
    sin = jnp.sin(ang).reshape(shape)
    xf = x.astype(jnp.float32)
    x1 = xf[..., :half]
    x2 = xf[..., half:rd]
    out = jnp.concatenate([x1 * cos - x2 * sin, x2 * cos + x1 * sin, xf[..., rd:]], axis=-1)
    return out.astype(x.dtype)


def dilated_window_attention(q, k, v, window, dilation):
    Bn, S, H, E = q.shape
    r = window // (2 * dilation)
    L = S // dilation
    nb = -(-L // r)
    Lp = nb * r

    def sub(a):
        return a.reshape(Bn, L, dilation, H, E).transpose(0, 2, 1, 3, 4)

    qb = jnp.pad(sub(q), ((0, 0), (0, 0), (0, Lp - L), (0, 0), (0, 0))).reshape(Bn, dilation, nb, r, H, E)

    def key_blocks(a):
        ap = jnp.pad(sub(a), ((0, 0), (0, 0), (r, Lp - L + r), (0, 0), (0, 0)))
        ap = ap.reshape(Bn, dilation, nb + 2, r, H, E)
        return jnp.concatenate([ap[:, :, :-2], ap[:, :, 1:-1], ap[:, :, 2:]], axis=3)

    kb = key_blocks(k)
    vb = key_blocks(v)
    qi = jnp.arange(nb)[:, None, None] * r + jnp.arange(r)[None, :, None]
    kj = (jnp.arange(nb)[:, None, None] - 1) * r + jnp.arange(3 * r)[None, None, :]
    valid = (jnp.abs(qi - kj) <= r) & (kj >= 0) & (kj < L)
    s = jnp.einsum('bdnqhe,bdnkhe->bdnhqk', qb, kb).astype(jnp.float32)
    s = jnp.where(valid[None, None, :, None], s, -jnp.inf)
    m = jnp.max(s, axis=-1, keepdims=True)
    p = jnp.exp(s - m)
    den = jnp.sum(p, axis=-1, keepdims=True)
    o = jnp.einsum('bdnhqk,bdnkhe->bdnqhe', (p / den).astype(v.dtype), vb)
    lse = (m + jnp.log(den))[..., 0]
    o = o.reshape(Bn, dilation, Lp, H, E)[:, :, :L].transpose(0, 2, 1, 3, 4).reshape(Bn, S, H, E)
    lse = lse.transpose(0, 1, 2, 4, 3).reshape(Bn, dilation, Lp, H)[:, :, :L]
    lse = lse.transpose(0, 2, 1, 3).reshape(Bn, S, H)
    return o, lse


def mixer_dilated(qkv, qn_g, kn_g, ang):
    Bn, S, _ = qkv.shape
    qkv = qkv.reshape(Bn, S, 3, N_A_GROUPS, A_HEADS, HEAD_DIM)
    q = rmsnorm(qkv[:, :, 0], qn_g[:, None, :])
    k = rmsnorm(qkv[:, :, 1], kn_g[:, None, :])
    v = qkv[:, :, 2]
    q = apply_rope(q, ang) * (HEAD_DIM ** -0.5)
    k = apply_rope(k, ang)
    outs, lses = [], []
    for g, (window, dilation) in enumerate(A_GROUPS):
        o_g, l_g = dilated_window_attention(q[:, :, g], k[:, :, g], v[:, :, g], window, dilation)
        outs.append(o_g)
        lses.append(l_g)
    o = jnp.stack(outs, axis=2)
    w = jax.nn.softmax(jnp.stack(lses, axis=2), axis=2)
    o = jnp.sum(w[..., None].astype(o.dtype) * o, axis=2)
    return o.reshape(Bn, S, A_W)


def mixer_axial_gqa(q, k, v, qn_g, kn_g, ang):
    Bn, S, _ = q.shape
    q = rmsnorm(q.reshape(Bn, S, B_Q_HEADS, HEAD_DIM), qn_g)
    k = rmsnorm(k.reshape(Bn, S, B_KV_HEADS, HEAD_DIM), kn_g)
    v = v.reshape(Bn, S, B_KV_HEADS, HEAD_DIM)
    q = apply_rope(q, ang) * (HEAD_DIM ** -0.5)
    k = apply_rope(k, ang)
    n_qb = S // Q_BLOCK
    qb = q.reshape(Bn, n_qb, Q_BLOCK, B_KV_HEADS, B_GROUP, HEAD_DIM).transpose(1, 0, 2, 3, 4, 5)

    def block(qi):
        s = jnp.einsum('bqhge,bshe->bhgqs', qi, k).astype(jnp.float32)
        p = jax.nn.softmax(s, axis=-1).astype(v.dtype)
        return jnp.einsum('bhgqs,bshe->bqhge', p, v)

    o = lax.map(block, qb)
    return o.transpose(1, 0, 2, 3, 4, 5).reshape(Bn, S, B_W)


def memory_attention(q_mem, mem_h, w_kv, qn_g, kn_g):
    Bn, S, _ = q_mem.shape
    N = mem_h.shape[1]
    q = rmsnorm(q_mem.reshape(Bn, S, MEM_HEADS, HEAD_DIM), qn_g) * (HEAD_DIM ** -0.5)
    kv = jnp.einsum('bnd,de->bne', mem_h, w_kv).reshape(Bn, N, 2, MEM_HEADS, HEAD_DIM)
    k = rmsnorm(kv[:, :, 0], kn_g)
    v = kv[:, :, 1]
    s = jnp.einsum('bqhe,bkhe->bhqk', q, k).astype(jnp.float32)
    p = jax.nn.softmax(s, axis=-1).astype(v.dtype)
    return jnp.einsum('bhqk,bkhe->bqhe', p, v).reshape(Bn, S, MEM_W)


def setup_inputs(seed: int = 0) -> dict:
    key = jax.random.key(seed)
    ks = jax.random.split(key, 16)
    nrm = jax.random.normal
    f32 = jnp.float32
    return {
        'x': nrm(ks[0], (BATCH, SEQ, D_MODEL), f32),
        'mem': nrm(ks[1], (BATCH, N_MEM_TOK, D_MODEL), f32),
        'norm_g': 1.0 + 0.01 * nrm(ks[2], (DEPTH, D_MODEL), f32),
        'mem_norm_g': 1.0 + 0.01 * nrm(ks[3], (DEPTH, D_MODEL), f32),
        'w_mem_kv': nrm(ks[4], (DEPTH, D_MODEL, 2 * MEM_W), f32) * D_MODEL ** -0.5,
        'mem_qn_g': 1.0 + 0.01 * nrm(ks[5], (DEPTH, HEAD_DIM), f32),
        'mem_kn_g': 1.0 + 0.01 * nrm(ks[6], (DEPTH, HEAD_DIM), f32),
        'w_out': nrm(ks[7], (DEPTH, BRANCH_W, D_MODEL), f32) * BRANCH_W ** -0.5,
        'w_in_a': nrm(ks[8], (N_A, D_MODEL, IN_A), f32) * D_MODEL ** -0.5,
        'qn_a': 1.0 + 0.01 * nrm(ks[9], (N_A, N_A_GROUPS, HEAD_DIM), f32),
        'kn_a': 1.0 + 0.01 * nrm(ks[10], (N_A, N_A_GROUPS, HEAD_DIM), f32),
        'w_in_b': nrm(ks[11], (N_B, D_MODEL, IN_B), f32) * D_MODEL ** -0.5,
        'qn_b': 1.0 + 0.01 * nrm(ks[12], (N_B, HEAD_DIM), f32),
        'kn_b': 1.0 + 0.01 * nrm(ks[13], (N_B, HEAD_DIM), f32),
    }


def reference(x, mem, norm_g, mem_norm_g, w_mem_kv, mem_qn_g, mem_kn_g, w_out,
              w_in_a, qn_a, kn_a, w_in_b, qn_b, kn_b):
    S = x.shape[1]
    ROWS = S // GRID_W
    pos = jnp.arange(S, dtype=jnp.int32)
    row = jnp.repeat(jnp.arange(ROWS, dtype=jnp.int32), GRID_W)
    col = jnp.tile(jnp.arange(GRID_W, dtype=jnp.int32), ROWS)
    ang_a = rope_angles(pos, ROT_DIM_A, ROPE_THETA)
    ang_b = jnp.concatenate([rope_angles(row, HEAD_DIM // 2, AXIAL_THETA),
                             rope_angles(col, HEAD_DIM // 2, AXIAL_THETA)], axis=-1)
    for i in range(DEPTH):
        h = rmsnorm(x, norm_g[i])
        mem_h = rmsnorm(mem, mem_norm_g[i])
        j = i // 2
        if i % 2 == 0:
            proj = jnp.einsum('bsd,de->bse', h, w_in_a[j])
            n_qkv = 3 * N_A_GROUPS * A_W
            qkv = proj[..., :n_qkv]
            q_mem = proj[..., n_qkv:n_qkv + MEM_W]
            gate = proj[..., n_qkv + MEM_W:]
            o_mix = mixer_dilated(qkv, qn_a[j], kn_a[j], ang_a)
        else:
            proj = jnp.einsum('bsd,de->bse', h, w_in_b[j])
            c1 = B_W
            c2 = c1 + B_KV_W
            c3 = c2 + B_KV_W
            c4 = c3 + MEM_W
            o_mix = mixer_axial_gqa(proj[..., :c1], proj[..., c1:c2], proj[..., c2:c3],
                                    qn_b[j], kn_b[j], ang_b)
            q_mem = proj[..., c3:c4]
            gate = proj[..., c4:]
        o_mem = memory_attention(q_mem, mem_h, w_mem_kv[i], mem_qn_g[i], mem_kn_g[i])
        y = jnp.concatenate([o_mix, o_mem], axis=-1) * jax.nn.silu(gate)
        x = x + jnp.einsum('bse,ed->bsd', y, w_out[i])
    return x
```

```python
import functools

import jax
import jax.numpy as jnp
from jax import lax
from jax.experimental import pallas as pl
from jax.experimental.pallas import tpu as pltpu

F32 = jnp.float32
BF16 = jnp.bfloat16

HEAD_DIM = 128
N_MEM_TOK = 256
MEM_HEADS = 4
MEM_W = MEM_HEADS * HEAD_DIM
A_GROUPS = ((128, 1), (512, 4), (2048, 16))
A_HEADS = 8
A_W = A_HEADS * HEAD_DIM
A_R = 64
B_Q_HEADS = 8
B_KV_HEADS = 2
B_GROUP = B_Q_HEADS // B_KV_HEADS
B_W = B_Q_HEADS * HEAD_DIM
B_KV_W = B_KV_HEADS * HEAD_DIM
BRANCH_W = A_W + MEM_W
ROPE_THETA = 500000.0
ROT_DIM_A = HEAD_DIM // 4
AXIAL_THETA = 10000.0
GRID_W = 64
EPS = 1e-6
NEG = -1e30

COL_TILE = 512
VMEM_LIMIT = 56 * 1024 * 1024


def _rms(x):
    return x * lax.rsqrt(jnp.mean(x * x, axis=-1, keepdims=True) + EPS)


def _nt_dot(a, b):
    return lax.dot_general(a, b, (((1,), (1,)), ((), ())), preferred_element_type=F32)


def _inproj_kernel(x_ref, g_ref, w_ref, gain_ref, cos_ref, s1_ref, s2_ref, o_ref, h_ref, *, kinds, shifts):
    j = pl.program_id(1)

    @pl.when(j == 0)
    def _():
        h_ref[...] = (_rms(x_ref[...]) * g_ref[...]).astype(BF16)

    acc = jnp.dot(h_ref[...], w_ref[...], preferred_element_type=F32)
    n_chunks = acc.shape[1] // HEAD_DIM

    def chunk(c):
        return acc[:, c * HEAD_DIM:(c + 1) * HEAD_DIM]

    def put(c, val):
        o_ref[:, c * HEAD_DIM:(c + 1) * HEAD_DIM] = val.astype(BF16)

    def normed(c, rope):
        a = _rms(chunk(c)) * gain_ref[...]
        if rope:
            r = a * cos_ref[...] + pltpu.roll(a, shifts[0], 1) * s1_ref[...]
            if len(shifts) > 1:
                r = r + pltpu.roll(a, shifts[1], 1) * s2_ref[...]
            a = r
        return a

    for lo, hi, kind in kinds:
        @pl.when((j >= lo) & (j < hi))
        def _(kind=kind):
            for c in range(n_chunks):
                if kind == "rope":
                    put(c, normed(c, True))
                elif kind == "norm":
                    put(c, normed(c, False))
                elif kind == "rope_then_plain":
                    put(c, normed(c, True) if c < n_chunks // 2 else chunk(c))
                elif kind == "silu":
                    a = chunk(c)
                    put(c, a / (1.0 + jnp.exp(-a)))
                else:
                    put(c, chunk(c))


def _inproj(x2, g, w, gains, tables, shifts, kinds, seq, tm):
    m, d = x2.shape
    n = w.shape[1]
    tn = COL_TILE
    assert m % tm == 0 and seq % tm == 0 and n % tn == 0
    pos_blocks = seq // tm
    tab_spec = pl.BlockSpec((tm, HEAD_DIM), lambda i, j: (i % pos_blocks, 0))
    return pl.pallas_call(
        functools.partial(_inproj_kernel, kinds=kinds, shifts=shifts),
        grid=(m // tm, n // tn),
        in_specs=[
            pl.BlockSpec((tm, d), lambda i, j: (i, 0)),
            pl.BlockSpec((1, d), lambda i, j: (0, 0)),
            pl.BlockSpec((d, tn), lambda i, j: (0, j)),
            pl.BlockSpec((None, 1, HEAD_DIM), lambda i, j: (j, 0, 0)),
            tab_spec, tab_spec, tab_spec,
        ],
        out_specs=pl.BlockSpec((tm, tn), lambda i, j: (i, j)),
        out_shape=jax.ShapeDtypeStruct((m, n), BF16),
        scratch_shapes=[pltpu.VMEM((tm, d), BF16)],
        compiler_params=pltpu.CompilerParams(
            dimension_semantics=("arbitrary", "arbitrary"), vmem_limit_bytes=VMEM_LIMIT),
        name="inproj",
    )(x2, g, w, gains, *tables)


def _memkv_kernel(mem_ref, g_ref, w_ref, kn_ref, o_ref):
    h = (_rms(mem_ref[...]) * g_ref[...]).astype(BF16)
    kv = jnp.dot(h, w_ref[...], preferred_element_type=F32)
    for c in range(MEM_HEADS):
        sl = slice(c * HEAD_DIM, (c + 1) * HEAD_DIM)
        o_ref[:, sl] = (_rms(kv[:, sl]) * kn_ref[...]).astype(BF16)
    o_ref[:, MEM_W:] = kv[:, MEM_W:].astype(BF16)


def _memkv(mem2, mem_norm_g, w_mem_kv, mem_kn_g):
    depth, d = mem_norm_g.shape
    rows = mem2.shape[0]
    return pl.pallas_call(
        _memkv_kernel,
        grid=(depth,),
        in_specs=[
            pl.BlockSpec((rows, d), lambda l: (0, 0)),
            pl.BlockSpec((None, 1, d), lambda l: (l, 0, 0)),
            pl.BlockSpec((None, d, 2 * MEM_W), lambda l: (l, 0, 0)),
            pl.BlockSpec((None, 1, HEAD_DIM), lambda l: (l, 0, 0)),
        ],
        out_specs=pl.BlockSpec((None, rows, 2 * MEM_W), lambda l: (l, 0, 0)),
        out_shape=jax.ShapeDtypeStruct((depth, rows, 2 * MEM_W), BF16),
        compiler_params=pltpu.CompilerParams(
            dimension_semantics=("arbitrary",), vmem_limit_bytes=VMEM_LIMIT),
        name="memkv",
    )(mem2, mem_norm_g.reshape(depth, 1, d), w_mem_kv, mem_kn_g.reshape(depth, 1, HEAD_DIM))


def _dilated_kernel(*refs, length, tq, has_prev, emit_lse):
    q_ref, kp_ref, kc_ref, kn_ref, vp_ref, vc_ref, vn_ref = refs[:7]
    pos = 7
    if has_prev:
        op_ref, lp_ref = refs[pos:pos + 2]
        pos += 2
    o_ref = refs[pos]
    l_ref = refs[pos + 1] if emit_lse else None

    n = pl.program_id(2)
    r = A_R
    nsub = tq // r
    qa = lax.broadcasted_iota(jnp.int32, (r, 3 * r), 0)
    kc_i = lax.broadcasted_iota(jnp.int32, (r, 3 * r), 1)
    band = jnp.abs(kc_i - r - qa) <= r
    lane = lax.broadcasted_iota(jnp.int32, (r, HEAD_DIM), 1)

    def window(p_ref, c_ref, n_ref, s, hs):
        if nsub == 1:
            return jnp.concatenate([p_ref[:, hs], c_ref[:, hs], n_ref[:, hs]], axis=0)
        if s == 0:
            return jnp.concatenate([p_ref[:, hs], c_ref[0:2 * r, hs]], axis=0)
        if s == nsub - 1:
            return jnp.concatenate([c_ref[tq - 2 * r:tq, hs], n_ref[:, hs]], axis=0)
        return c_ref[(s - 1) * r:(s + 2) * r, hs]

    for s in range(nsub):
        rows = slice(s * r, (s + 1) * r)
        key0 = n * tq + (s - 1) * r
        valid = band
        if s == 0:
            valid = valid & (kc_i + key0 >= 0)
        if s == nsub - 1:
            valid = valid & (kc_i + key0 < length)
        lse_tile = jnp.zeros((r, HEAD_DIM), F32)
        if has_prev:
            lp_tile = lp_ref[rows, :]
        for h in range(A_HEADS):
            hs = slice(h * HEAD_DIM, (h + 1) * HEAD_DIM)
            kw = window(kp_ref, kc_ref, kn_ref, s, hs)
            vw = window(vp_ref, vc_ref, vn_ref, s, hs)
            sc = jnp.where(valid, _nt_dot(q_ref[rows, hs], kw), NEG)
            m = jnp.max(sc, axis=-1, keepdims=True)
            p = jnp.exp(sc - m)
            den = jnp.sum(p, axis=-1, keepdims=True)
            o = jnp.dot(p.astype(BF16), vw, preferred_element_type=F32) / den
            lse = m + jnp.log(den)
            if has_prev:
                lp = lp_tile[:, h:h + 1]
                m2 = jnp.maximum(lp, lse)
                w0 = jnp.exp(lp - m2)
                w1 = jnp.exp(lse - m2)
                o = (w0 * op_ref[rows, hs].astype(F32) + w1 * o) / (w0 + w1)
                lse = m2 + jnp.log(w0 + w1)
            o_ref[rows, hs] = o.astype(o_ref.dtype)
            if emit_lse:
                lse_tile = jnp.where(lane == h, lse, lse_tile)
        if emit_lse:
            l_ref[rows, :] = lse_tile


def _dilated_group(proj3, prev, g, dilation, n_cols, emit_lse):
    bsz, seq, _ = proj3.shape
    d = dilation
    length = seq // d
    r = A_R
    assert length % r == 0
    tq = min(length, 4 * r)
    assert length % tq == 0
    nblk = n_cols // A_W
    n_groups = len(A_GROUPS)
    view = proj3.reshape(bsz, length, d * n_cols)
    sub = tq // r
    last_r = length // r - 1

    def col(kind):
        return lambda b, res, n: res * nblk + kind * n_groups + g

    def cur_spec(kind):
        c = col(kind)
        return pl.BlockSpec((None, tq, A_W), lambda b, res, n: (b, n, c(b, res, n)))

    def prev_spec(kind):
        c = col(kind)
        return pl.BlockSpec((None, r, A_W), lambda b, res, n: (b, jnp.maximum(n * sub - 1, 0), c(b, res, n)))

    def next_spec(kind):
        c = col(kind)
        return pl.BlockSpec((None, r, A_W), lambda b, res, n: (b, jnp.minimum((n + 1) * sub, last_r), c(b, res, n)))

    o_spec = pl.BlockSpec((None, tq, A_W), lambda b, res, n: (b, n, res))
    l_spec = pl.BlockSpec((None, tq, HEAD_DIM), lambda b, res, n: (b, n, res))
    in_specs = [cur_spec(0), prev_spec(1), cur_spec(1), next_spec(1), prev_spec(2), cur_spec(2), next_spec(2)]
    args = [view] * 7
    if prev is not None:
        in_specs += [o_spec, l_spec]
        args += [prev[0].reshape(bsz, length, d * A_W), prev[1].reshape(bsz, length, d * HEAD_DIM)]
    out_specs = [o_spec]
    out_shape = [jax.ShapeDtypeStruct((bsz, length, d * A_W), BF16)]
    if emit_lse:
        out_specs.append(l_spec)
        out_shape.append(jax.ShapeDtypeStruct((bsz, length, d * HEAD_DIM), F32))
    outs = pl.pallas_call(
        functools.partial(_dilated_kernel, length=length, tq=tq, has_prev=prev is not None, emit_lse=emit_lse),
        grid=(bsz, d, length // tq),
        in_specs=in_specs,
        out_specs=out_specs,
        out_shape=out_shape,
        compiler_params=pltpu.CompilerParams(
            dimension_semantics=("arbitrary", "arbitrary", "arbitrary"), vmem_limit_bytes=VMEM_LIMIT),
        name=f"dilated_g{g}",
    )(*args)
    o = outs[0].reshape(bsz, seq, A_W)
    if emit_lse:
        return o, outs[1].reshape(bsz, seq, HEAD_DIM)
    return o, None


def _gqa_kernel(q_ref, k_ref, v_ref, o_ref, *, tk):
    seq = k_ref.shape[0]
    tq = q_ref.shape[0]
    for h in range(B_GROUP):
        hs = slice(h * HEAD_DIM, (h + 1) * HEAD_DIM)
        q = q_ref[:, hs]

        def body(c, carry):
            m, l, acc = carry
            start = pl.multiple_of(c * tk, tk)
            k = k_ref[pl.ds(start, tk), :]
            v = v_ref[pl.ds(start, tk), :]
            s = _nt_dot(q, k)
            m_new = jnp.maximum(m, jnp.max(s, axis=-1, keepdims=True))
            alpha = jnp.exp(m - m_new)
            p = jnp.exp(s - m_new)
            l = alpha * l + jnp.sum(p, axis=-1, keepdims=True)
            acc = alpha * acc + jnp.dot(p.astype(BF16), v, preferred_element_type=F32)
            return m_new, l, acc

        init = (jnp.full((tq, 1), NEG, F32), jnp.zeros((tq, 1), F32), jnp.zeros((tq, HEAD_DIM), F32))
        _, l, acc = lax.fori_loop(0, seq // tk, body, init)
        o_ref[:, hs] = (acc / l).astype(o_ref.dtype)


def _gqa(proj3, tq, tk):
    bsz, seq, _ = proj3.shape
    assert seq % tq == 0 and seq % tk == 0
    qw = B_GROUP * HEAD_DIM
    k_blk = B_W // HEAD_DIM
    v_blk = (B_W + B_KV_W) // HEAD_DIM
    return pl.pallas_call(
        functools.partial(_gqa_kernel, tk=tk),
        grid=(bsz, B_KV_HEADS, seq // tq),
        in_specs=[
            pl.BlockSpec((None, tq, qw), lambda b, kh, i: (b, i, kh)),
            pl.BlockSpec((None, seq, HEAD_DIM), lambda b, kh, i: (b, 0, k_blk + kh)),
            pl.BlockSpec((None, seq, HEAD_DIM), lambda b, kh, i: (b, 0, v_blk + kh)),
        ],
        out_specs=pl.BlockSpec((None, tq, qw), lambda b, kh, i: (b, i, kh)),
        out_shape=jax.ShapeDtypeStruct((bsz, seq, B_W), BF16),
        compiler_params=pltpu.CompilerParams(
            dimension_semantics=("arbitrary", "arbitrary", "arbitrary"), vmem_limit_bytes=VMEM_LIMIT),
        name="gqa",
    )(proj3, proj3, proj3)


def _outproj_kernel(x_ref, om_ref, qm_ref, g0_ref, g1_ref, g2_ref, mk_ref, mv_ref, w_ref, o_ref):
    mem_heads = []
    for h in range(MEM_HEADS):
        hs = slice(h * HEAD_DIM, (h + 1) * HEAD_DIM)
        s = _nt_dot(qm_ref[:, hs], mk_ref[:, hs])
        p = jnp.exp(s - jnp.max(s, axis=-1, keepdims=True))
        den = jnp.sum(p, axis=-1, keepdims=True)
        mem_heads.append(jnp.dot(p.astype(BF16), mv_ref[:, hs], preferred_element_type=F32) / den)
    o_mem = jnp.concatenate(mem_heads, axis=-1)

    half = COL_TILE
    y0 = (om_ref[:, :half].astype(F32) * g0_ref[...].astype(F32)).astype(BF16)
    y1 = (om_ref[:, half:].astype(F32) * g1_ref[...].astype(F32)).astype(BF16)
    y2 = (o_mem * g2_ref[...].astype(F32)).astype(BF16)
    acc = jnp.dot(y0, w_ref[0:half, :], preferred_element_type=F32)
    acc = acc + jnp.dot(y1, w_ref[half:2 * half, :], preferred_element_type=F32)
    acc = acc + jnp.dot(y2, w_ref[2 * half:3 * half, :], preferred_element_type=F32)
    o_ref[...] = x_ref[...] + acc


def _outproj(x2, o_mix2, proj2, memkv_l, w_out, qm_col, seq, tm):
    m, d = x2.shape
    assert m % tm == 0 and seq % tm == 0 and qm_col % COL_TILE == 0 and A_W == 2 * COL_TILE and MEM_W == COL_TILE
    tiles_per_batch = seq // tm
    qb = qm_col // COL_TILE

    def proj_spec(blk):
        return pl.BlockSpec((tm, COL_TILE), lambda i: (i, blk))

    return pl.pallas_call(
        _outproj_kernel,
        grid=(m // tm,),
        in_specs=[
            pl.BlockSpec((tm, d), lambda i: (i, 0)),
            pl.BlockSpec((tm, A_W), lambda i: (i, 0)),
            proj_spec(qb), proj_spec(qb + 1), proj_spec(qb + 2), proj_spec(qb + 3),
            pl.BlockSpec((N_MEM_TOK, MEM_W), lambda i: (i // tiles_per_batch, 0)),
            pl.BlockSpec((N_MEM_TOK, MEM_W), lambda i: (i // tiles_per_batch, 1)),
            pl.BlockSpec((BRANCH_W, d), lambda i: (0, 0)),
        ],
        out_specs=pl.BlockSpec((tm, d), lambda i: (i, 0)),
        out_shape=jax.ShapeDtypeStruct((m, d), F32),
        compiler_params=pltpu.CompilerParams(
            dimension_semantics=("arbitrary",), vmem_limit_bytes=VMEM_LIMIT),
        name="outproj",
    )(x2, o_mix2, proj2, proj2, proj2, proj2, memkv_l, memkv_l, w_out)


def _rope_angles(pos, dim, theta):
    inv = theta ** (-jnp.arange(0, dim, 2, dtype=F32) / dim)
    return pos.astype(F32)[:, None] * inv[None, :]


def _tables_a(seq):
    half = ROT_DIM_A // 2
    ang = _rope_angles(jnp.arange(seq, dtype=jnp.int32), ROT_DIM_A, ROPE_THETA)
    cos, sin = jnp.cos(ang), jnp.sin(ang)
    rest = HEAD_DIM - ROT_DIM_A
    z = jnp.zeros((seq, half), F32)
    cos_t = jnp.concatenate([cos, cos, jnp.ones((seq, rest), F32)], axis=-1)
    s_up = jnp.concatenate([z, sin, jnp.zeros((seq, rest), F32)], axis=-1)
    s_dn = jnp.concatenate([-sin, z, jnp.zeros((seq, rest), F32)], axis=-1)
    return (cos_t, s_up, s_dn), (half, HEAD_DIM - half)


def _tables_b(seq):
    rows = seq // GRID_W
    row = jnp.repeat(jnp.arange(rows, dtype=jnp.int32), GRID_W)
    col = jnp.tile(jnp.arange(GRID_W, dtype=jnp.int32), rows)
    ang = jnp.concatenate([_rope_angles(row, HEAD_DIM // 2, AXIAL_THETA),
                           _rope_angles(col, HEAD_DIM // 2, AXIAL_THETA)], axis=-1)
    cos, sin = jnp.cos(ang), jnp.sin(ang)
    cos_t = jnp.concatenate([cos, cos], axis=-1)
    s_t = jnp.concatenate([-sin, sin], axis=-1)
    return (cos_t, s_t, s_t), (HEAD_DIM // 2,)


def _tile_gains(kinds_gains, n_tiles):
    rows = []
    for lo, hi, gain in kinds_gains:
        rows += [gain] * (hi - lo)
    assert len(rows) == n_tiles
    return jnp.stack(rows).reshape(n_tiles, 1, HEAD_DIM)


def kernel(x, mem, norm_g, mem_norm_g, w_mem_kv, mem_qn_g, mem_kn_g, w_out, w_in_a, qn_a, kn_a, w_in_b, qn_b, kn_b):
    bsz, seq, d = x.shape
    depth = norm_g.shape[0]
    m = bsz * seq
    scale = HEAD_DIM ** -0.5
    ones = jnp.ones((HEAD_DIM,), F32)
    tm = min(seq, 1024)
    per = COL_TILE // HEAD_DIM
    tiles_a = A_W // COL_TILE

    memkv = _memkv(mem.reshape(bsz * N_MEM_TOK, d), mem_norm_g, w_mem_kv.astype(BF16), mem_kn_g)
    tab_a, shifts_a = _tables_a(seq)
    tab_b, shifts_b = _tables_b(seq)

    x2 = x.reshape(m, d)
    for i in range(depth):
        j = i // 2
        g_row = norm_g[i].reshape(1, d)
        mem_q_gain = mem_qn_g[i] * scale
        if i % 2 == 0:
            w = w_in_a[j].astype(BF16)
            n_cols = w.shape[1]
            ng = len(A_GROUPS)
            nq = ng * tiles_a
            kinds = ((0, 2 * nq, "rope"), (2 * nq, 3 * nq, "plain"),
                     (3 * nq, 3 * nq + 1, "norm"), (3 * nq + 1, 3 * nq + 4, "silu"))
            gains = []
            for g in range(ng):
                gains.append((g * tiles_a, (g + 1) * tiles_a, qn_a[j, g] * scale))
            for g in range(ng):
                gains.append((nq + g * tiles_a, nq + (g + 1) * tiles_a, kn_a[j, g]))
            gains += [(2 * nq, 3 * nq, ones), (3 * nq, 3 * nq + 1, mem_q_gain), (3 * nq + 1, 3 * nq + 4, ones)]
            proj = _inproj(x2, g_row, w, _tile_gains(gains, n_cols // COL_TILE), tab_a, shifts_a, kinds, seq, tm)
            proj3 = proj.reshape(bsz, seq, n_cols)
            prev = None
            for g, (_, dilation) in enumerate(A_GROUPS):
                o_mix, lse = _dilated_group(proj3, prev, g, dilation, n_cols, emit_lse=g < ng - 1)
                prev = (o_mix, lse)
            qm_col = 3 * ng * A_W
        else:
            w = w_in_b[j].astype(BF16)
            n_cols = w.shape[1]
            nq = B_W // COL_TILE
            assert 2 * B_KV_W == COL_TILE
            kinds = ((0, nq, "rope"), (nq, nq + 1, "rope_then_plain"),
                     (nq + 1, nq + 2, "norm"), (nq + 2, nq + 5, "silu"))
            gains = [(0, nq, qn_b[j] * scale), (nq, nq + 1, kn_b[j]),
                     (nq + 1, nq + 2, mem_q_gain), (nq + 2, nq + 5, ones)]
            proj = _inproj(x2, g_row, w, _tile_gains(gains, n_cols // COL_TILE), tab_b, shifts_b, kinds, seq, tm)
            o_mix = _gqa(proj.reshape(bsz, seq, n_cols), tq=min(seq, 512), tk=min(seq, 512))
            qm_col = B_W + 2 * B_KV_W
        x2 = _outproj(x2, o_mix.reshape(m, A_W), proj, memkv[i], w_out[i].astype(BF16), qm_col, seq, min(seq, 512))
    return x2.reshape(bsz, seq, d)
```

```python
import functools
import math

import jax
import jax.numpy as jnp
from jax import lax
from jax.experimental import pallas as pl
from jax.experimental.pallas import tpu as pltpu

F32 = jnp.float32
BF16 = jnp.bfloat16

HEAD_DIM = 128
N_MEM_TOK = 256
MEM_HEADS = 4
MEM_W = MEM_HEADS * HEAD_DIM
A_GROUPS = ((128, 1), (512, 4), (2048, 16))
A_HEADS = 8
A_W = A_HEADS * HEAD_DIM
A_R = 64
B_Q_HEADS = 8
B_KV_HEADS = 2
B_GROUP = B_Q_HEADS // B_KV_HEADS
B_W = B_Q_HEADS * HEAD_DIM
B_KV_W = B_KV_HEADS * HEAD_DIM
BRANCH_W = A_W + MEM_W
ROPE_THETA = 500000.0
ROT_DIM_A = HEAD_DIM // 4
AXIAL_THETA = 10000.0
GRID_W = 64
EPS = 1e-6
NEG = -1e30
Q_SCALE = HEAD_DIM ** -0.5 * math.log2(math.e)

COL_TILE = 512
VMEM_LIMIT = 56 * 1024 * 1024
DIL_CHUNK = 256
DIL_CHUNKS_PER_TRIP = 2
DIL_UNROLL = 16


def _rms(x):
    return x * lax.rsqrt(jnp.mean(x * x, axis=-1, keepdims=True) + EPS)


def _nt_dot(a, b):
    return lax.dot_general(a, b, (((1,), (1,)), ((), ())), preferred_element_type=F32)


def _params(n_axes):
    return pltpu.CompilerParams(dimension_semantics=("arbitrary",) * n_axes, vmem_limit_bytes=VMEM_LIMIT)


def _rmsnorm_kernel(x_ref, g_ref, o_ref):
    o_ref[...] = (_rms(x_ref[...]) * g_ref[...]).astype(BF16)


def _rmsnorm(x2, g_row, tm):
    m, d = x2.shape
    return pl.pallas_call(
        _rmsnorm_kernel,
        grid=(m // tm,),
        in_specs=[pl.BlockSpec((tm, d), lambda i: (i, 0)), pl.BlockSpec((1, d), lambda i: (0, 0))],
        out_specs=pl.BlockSpec((tm, d), lambda i: (i, 0)),
        out_shape=jax.ShapeDtypeStruct((m, d), BF16),
        compiler_params=_params(1),
        name="rmsnorm",
    )(x2, g_row)


def _proj_kernel(h_ref, w_ref, gain_ref, cos_ref, sin_ref, ones_ref, o_ref, *, kinds):
    per = COL_TILE // HEAD_DIM
    half = HEAD_DIM // 2
    for t, kind in enumerate(kinds):
        acc = jnp.dot(h_ref[...], w_ref[:, t * COL_TILE:(t + 1) * COL_TILE], preferred_element_type=F32)
        gain = gain_ref[t, 0:1, :]
        if kind in ("rope", "rope_then_plain"):
            g_cos = cos_ref[...] * gain
            g_sin = sin_ref[...] * gain_ref[t, 1:2, :]
            sumsq = jnp.dot((acc * acc).astype(BF16), ones_ref[...], preferred_element_type=F32)
            rstd_all = lax.rsqrt(sumsq * (1.0 / HEAD_DIM) + EPS)
        for c in range(per):
            a = acc[:, c * HEAD_DIM:(c + 1) * HEAD_DIM]
            if kind == "rope" or (kind == "rope_then_plain" and c < per // 2):
                rstd = rstd_all[:, c * HEAD_DIM:(c + 1) * HEAD_DIM]
                a = (a * g_cos + pltpu.roll(a, half, 1) * g_sin) * rstd
            elif kind == "norm":
                a = _rms(a) * gain
            elif kind == "silu":
                a = a / (1.0 + jnp.exp(-a))
            lo = t * COL_TILE + c * HEAD_DIM
            o_ref[:, lo:lo + HEAD_DIM] = a.astype(BF16)


def _proj(h2, w, gains, tables, kinds, seq, tm, name):
    m, d = h2.shape
    n = w.shape[1]
    tn = len(kinds) * COL_TILE
    assert m % tm == 0 and seq % tm == 0 and n % tn == 0
    pos_blocks = seq // tm
    tab_spec = pl.BlockSpec((tm, HEAD_DIM), lambda i, j: (i % pos_blocks, 0))
    lane_head = jnp.arange(COL_TILE, dtype=jnp.int32) // HEAD_DIM
    head_ones = (lane_head[:, None] == lane_head[None, :]).astype(BF16)
    return pl.pallas_call(
        functools.partial(_proj_kernel, kinds=kinds),
        grid=(m // tm, n // tn),
        in_specs=[
            pl.BlockSpec((tm, d), lambda i, j: (i, 0)),
            pl.BlockSpec((d, tn), lambda i, j: (0, j)),
            pl.BlockSpec((len(kinds), 2, HEAD_DIM), lambda i, j: (j, 0, 0)),
            tab_spec, tab_spec,
            pl.BlockSpec((COL_TILE, COL_TILE), lambda i, j: (0, 0)),
        ],
        out_specs=pl.BlockSpec((tm, tn), lambda i, j: (i, j)),
        out_shape=jax.ShapeDtypeStruct((m, n), BF16),
        compiler_params=_params(2),
        name=name,
    )(h2, w, gains, *tables, head_ones)


def _granule_gains(rows):
    g = jnp.stack(rows)
    return jnp.stack([g, jnp.roll(g, HEAD_DIM // 2, axis=-1)], axis=1)


def _memkv_kernel(mem_ref, g_ref, w_ref, kn_ref, o_ref):
    h = (_rms(mem_ref[...]) * g_ref[...]).astype(BF16)
    kv = jnp.dot(h, w_ref[...], preferred_element_type=F32)
    for c in range(MEM_HEADS):
        sl = slice(c * HEAD_DIM, (c + 1) * HEAD_DIM)
        o_ref[:, sl] = (_rms(kv[:, sl]) * kn_ref[...]).astype(BF16)
    o_ref[:, MEM_W:] = kv[:, MEM_W:].astype(BF16)


def _memkv(mem2, mem_norm_g, w_mem_kv, mem_kn_g):
    depth, d = mem_norm_g.shape
    rows = mem2.shape[0]
    return pl.pallas_call(
        _memkv_kernel,
        grid=(depth,),
        in_specs=[
            pl.BlockSpec((rows, d), lambda l: (0, 0)),
            pl.BlockSpec((None, 1, d), lambda l: (l, 0, 0)),
            pl.BlockSpec((None, d, 2 * MEM_W), lambda l: (l, 0, 0)),
            pl.BlockSpec((None, 1, HEAD_DIM), lambda l: (l, 0, 0)),
        ],
        out_specs=pl.BlockSpec((None, rows, 2 * MEM_W), lambda l: (l, 0, 0)),
        out_shape=jax.ShapeDtypeStruct((depth, rows, 2 * MEM_W), BF16),
        compiler_params=_params(1),
        name="memkv",
    )(mem2, mem_norm_g.reshape(depth, 1, d), w_mem_kv, mem_kn_g.reshape(depth, 1, HEAD_DIM))


def _perm_matrix(d):
    n = DIL_CHUNK
    rp = n // d
    row = lax.broadcasted_iota(jnp.int32, (n, n), 0)
    tok = lax.broadcasted_iota(jnp.int32, (n, n), 1)
    src = jnp.bitwise_and(row, rp - 1) * d + lax.shift_right_logical(row, rp.bit_length() - 1)
    return jnp.where(tok == src, 1.0, 0.0).astype(BF16)


def _dilated_kernel(q0, k0, v0, q1, k1, v1, q2, k2, v2, o_ref,
                    qp_ref, kp_ref, vp_ref, oacc_ref, lacc_ref, bias_ref, *, seq):
    r = A_R
    qa = lax.broadcasted_iota(jnp.int32, (r, 3 * r), 0)
    kc = lax.broadcasted_iota(jnp.int32, (r, 3 * r), 1)
    band = jnp.abs(kc - r - qa) <= r
    left = kc >= r
    right = kc < 2 * r
    bias_ref[0] = jnp.where(band, 0.0, NEG)
    bias_ref[1] = jnp.where(band & left, 0.0, NEG)
    bias_ref[2] = jnp.where(band & right, 0.0, NEG)
    bias_ref[3] = jnp.where(band & left & right, 0.0, NEG)
    zeros = jnp.zeros((r, HEAD_DIM), BF16)

    for g, ((_, d), (q_ref, k_ref, v_ref)) in enumerate(zip(A_GROUPS, ((q0, k0, v0), (q1, k1, v1), (q2, k2, v2)))):
        length = seq // d
        nsub = length // r
        seg = length + 2 * r
        for res in range(d):
            for ref in (kp_ref, vp_ref):
                ref[res * seg:res * seg + r, :] = zeros
                ref[res * seg + r + length:(res + 1) * seg, :] = zeros
        if d == 1:
            kp_ref[r:r + seq, :] = k_ref[...]
            vp_ref[r:r + seq, :] = v_ref[...]
            q_src = q_ref
        else:
            perm = _perm_matrix(d)
            rp = DIL_CHUNK // d

            def chunk_body(it, carry, d=d, perm=perm, rp=rp, length=length, seg=seg,
                           q_ref=q_ref, k_ref=k_ref, v_ref=v_ref):
                for j in range(DIL_CHUNKS_PER_TRIP):
                    c = it * DIL_CHUNKS_PER_TRIP + j
                    rows = pl.ds(pl.multiple_of(c * DIL_CHUNK, DIL_CHUNK), DIL_CHUNK)
                    qkv = jnp.concatenate([q_ref[rows, :], k_ref[rows, :], v_ref[rows, :]], axis=1)
                    y = jnp.dot(perm, qkv, preferred_element_type=F32).astype(BF16)
                    for n, (dst, off, pitch) in enumerate(((qp_ref, 0, length), (kp_ref, r, seg), (vp_ref, r, seg))):
                        for res in range(d):
                            start = pl.multiple_of(res * pitch + off + c * rp, 16)
                            dst[pl.ds(start, rp), :] = y[res * rp:(res + 1) * rp, n * HEAD_DIM:(n + 1) * HEAD_DIM]
                return carry

            lax.fori_loop(0, seq // DIL_CHUNK // DIL_CHUNKS_PER_TRIP, chunk_body, 0)
            q_src = qp_ref

        shift = nsub.bit_length() - 1

        def unit_body(it, carry, g=g, d=d, nsub=nsub, shift=shift, q_src=q_src):
            blocks = []
            for j in range(DIL_UNROLL):
                u = it * DIL_UNROLL + j
                res = lax.shift_right_logical(u, shift)
                s = jnp.bitwise_and(u, nsub - 1)
                q_start = pl.multiple_of(u * r, r)
                k_start = pl.multiple_of(u * r + res * (2 * r), r)
                edge = jnp.where(s == 0, 1, 0) + jnp.where(s == nsub - 1, 2, 0)
                sc = _nt_dot(q_src[pl.ds(q_start, r), :], kp_ref[pl.ds(k_start, 3 * r), :]) + bias_ref[edge]
                blocks.append((res, s, k_start, sc))
            soft = []
            for res, s, k_start, sc in blocks:
                m = jnp.max(sc, axis=-1, keepdims=True)
                p = jnp.exp2(sc - m)
                den = jnp.sum(p, axis=-1, keepdims=True)
                soft.append((res, s, k_start, m, p, den))
            for res, s, k_start, m, p, den in soft:
                o = jnp.dot(p.astype(BF16), vp_ref[pl.ds(k_start, 3 * r), :], preferred_element_type=F32) / den
                lse = m + jnp.log2(den)
                tok0 = res + d * (s * r)
                rows = pl.ds(pl.multiple_of(tok0, r), r) if d == 1 else pl.ds(tok0, r, stride=d)
                if g > 0:
                    lp = lacc_ref[rows, :]
                    m2 = jnp.maximum(lp, lse)
                    w0 = jnp.exp2(lp - m2)
                    w1 = jnp.exp2(lse - m2)
                    tot = w0 + w1
                    o = (w0 * oacc_ref[rows, :] + w1 * o) / tot
                    lse = m2 + jnp.log2(tot)
                oacc_ref[rows, :] = o
                if g < len(A_GROUPS) - 1:
                    lacc_ref[rows, :] = jnp.broadcast_to(lse, (r, HEAD_DIM))
            return carry

        lax.fori_loop(0, seq // r // DIL_UNROLL, unit_body, 0)

    o_ref[...] = oacc_ref[...].astype(o_ref.dtype)


def _dilated(qk3, v3):
    bsz, seq, _ = qk3.shape
    ng = len(A_GROUPS)
    d_max = max(d for _, d in A_GROUPS)
    assert seq % (d_max * A_R) == 0 and (seq // A_R) % DIL_UNROLL == 0
    assert seq % (DIL_CHUNK * DIL_CHUNKS_PER_TRIP) == 0
    assert all((seq // d // A_R) & (seq // d // A_R - 1) == 0 for _, d in A_GROUPS)

    def head_spec(blk0):
        return pl.BlockSpec((None, seq, HEAD_DIM), lambda b, h: (b, 0, blk0 + h))

    in_specs, args = [], []
    for g in range(ng):
        in_specs += [head_spec(g * A_HEADS), head_spec((ng + g) * A_HEADS), head_spec(g * A_HEADS)]
        args += [qk3, qk3, v3]
    pad_rows = seq + 2 * A_R * d_max
    return pl.pallas_call(
        functools.partial(_dilated_kernel, seq=seq),
        grid=(bsz, A_HEADS),
        in_specs=in_specs,
        out_specs=pl.BlockSpec((None, seq, HEAD_DIM), lambda b, h: (b, 0, h)),
        out_shape=jax.ShapeDtypeStruct((bsz, seq, A_W), BF16),
        scratch_shapes=[
            pltpu.VMEM((seq, HEAD_DIM), BF16),
            pltpu.VMEM((pad_rows, HEAD_DIM), BF16),
            pltpu.VMEM((pad_rows, HEAD_DIM), BF16),
            pltpu.VMEM((seq, HEAD_DIM), F32),
            pltpu.VMEM((seq, HEAD_DIM), F32),
            pltpu.VMEM((4, A_R, 3 * A_R), F32),
        ],
        compiler_params=_params(2),
        name="dilated",
    )(*args)


def _gqa_kernel(q_ref, k_ref, v_ref, o_ref, qs_ref, va_ref, s_ref, p_ref, a_ref, m_ref, acc_ref, *, tk):
    tq = q_ref.shape[0]
    n = k_ref.shape[0] // tk

    @pl.when(pl.program_id(2) == 0)
    def _():
        va_ref[:, :HEAD_DIM] = v_ref[...]
        va_ref[:, HEAD_DIM:] = jnp.ones((va_ref.shape[0], HEAD_DIM), BF16)

    for h in range(B_GROUP):
        qs_ref[h * tq:(h + 1) * tq, :] = q_ref[:, h * HEAD_DIM:(h + 1) * HEAD_DIM]
    m_ref[...] = jnp.full(m_ref.shape, NEG, F32)
    acc_ref[...] = jnp.zeros(acc_ref.shape, F32)

    def scores(c, slot):
        s_ref[slot] = _nt_dot(qs_ref[...], k_ref[c * tk:(c + 1) * tk, :])

    def softmax(slot):
        blocks = [s_ref[slot, :, j * HEAD_DIM:(j + 1) * HEAD_DIM] for j in range(tk // HEAD_DIM)]
        part = blocks[0]
        for blk in blocks[1:]:
            part = jnp.maximum(part, blk)
        m = m_ref[...]
        m_new = jnp.maximum(m, jnp.max(part, axis=-1, keepdims=True))
        a_ref[slot] = jnp.exp2(m - m_new)
        m_ref[...] = m_new
        for j, blk in enumerate(blocks):
            p_ref[slot, :, j * HEAD_DIM:(j + 1) * HEAD_DIM] = jnp.exp2(blk - m_new).astype(BF16)

    def pv(c, slot):
        upd = jnp.dot(p_ref[slot], va_ref[c * tk:(c + 1) * tk, :], preferred_element_type=F32)
        alpha = a_ref[slot]
        for j in range(2):
            cols = slice(j * HEAD_DIM, (j + 1) * HEAD_DIM)
            acc_ref[:, cols] = alpha * acc_ref[:, cols] + upd[:, cols]

    scores(0, 0)
    scores(1, 1)
    softmax(0)

    for c in range(0, n - 2, 2):
        scores(c + 2, 0)
        softmax(1)
        pv(c, 0)
        scores(c + 3, 1)
        softmax(0)
        pv(c + 1, 1)
    softmax(1)
    pv(n - 2, 0)
    pv(n - 1, 1)
    out = acc_ref[:, :HEAD_DIM] / acc_ref[:, HEAD_DIM:]
    for h in range(B_GROUP):
        o_ref[:, h * HEAD_DIM:(h + 1) * HEAD_DIM] = out[h * tq:(h + 1) * tq, :].astype(o_ref.dtype)


def _gqa(proj3, tq, tk):
    bsz, seq, _ = proj3.shape
    assert seq % tq == 0 and seq % (2 * tk) == 0
    qw = B_GROUP * HEAD_DIM
    k_blk = B_W // HEAD_DIM
    v_blk = (B_W + B_KV_W) // HEAD_DIM
    rows = B_GROUP * tq
    return pl.pallas_call(
        functools.partial(_gqa_kernel, tk=tk),
        grid=(bsz, B_KV_HEADS, seq // tq),
        in_specs=[
            pl.BlockSpec((None, tq, qw), lambda b, kh, i: (b, i, kh)),
            pl.BlockSpec((None, seq, HEAD_DIM), lambda b, kh, i: (b, 0, k_blk + kh)),
            pl.BlockSpec((None, seq, HEAD_DIM), lambda b, kh, i: (b, 0, v_blk + kh)),
        ],
        out_specs=pl.BlockSpec((None, tq, qw), lambda b, kh, i: (b, i, kh)),
        out_shape=jax.ShapeDtypeStruct((bsz, seq, B_W), BF16),
        scratch_shapes=[
            pltpu.VMEM((rows, HEAD_DIM), BF16),
            pltpu.VMEM((seq, 2 * HEAD_DIM), BF16),
            pltpu.VMEM((2, rows, tk), F32),
            pltpu.VMEM((2, rows, tk), BF16),
            pltpu.VMEM((2, rows, HEAD_DIM), F32),
            pltpu.VMEM((rows, HEAD_DIM), F32),
            pltpu.VMEM((rows, 2 * HEAD_DIM), F32),
        ],
        compiler_params=_params(3),
        name="gqa",
    )(proj3, proj3, proj3)


def _outproj_kernel(x_ref, om_ref, qm_ref, g0_ref, g1_ref, g2_ref, mk_ref, mv_ref, w_ref, o_ref):
    mem_heads = []
    for h in range(MEM_HEADS):
        hs = slice(h * HEAD_DIM, (h + 1) * HEAD_DIM)
        s = _nt_dot(qm_ref[:, hs], mk_ref[:, hs])
        p = jnp.exp2(s - jnp.max(s, axis=-1, keepdims=True))
        den = jnp.sum(p, axis=-1, keepdims=True)
        mem_heads.append(jnp.dot(p.astype(BF16), mv_ref[:, hs], preferred_element_type=F32) / den)
    o_mem = jnp.concatenate(mem_heads, axis=-1)

    half = COL_TILE
    y0 = (om_ref[:, :half].astype(F32) * g0_ref[...].astype(F32)).astype(BF16)
    y1 = (om_ref[:, half:].astype(F32) * g1_ref[...].astype(F32)).astype(BF16)
    y2 = (o_mem * g2_ref[...].astype(F32)).astype(BF16)
    acc = jnp.dot(y0, w_ref[0:half, :], preferred_element_type=F32)
    acc = acc + jnp.dot(y1, w_ref[half:2 * half, :], preferred_element_type=F32)
    acc = acc + jnp.dot(y2, w_ref[2 * half:3 * half, :], preferred_element_type=F32)
    o_ref[...] = x_ref[...] + acc


def _outproj(x2, o_mix2, proj2, memkv_l, w_out, qm_col, seq, tm):
    m, d = x2.shape
    assert m % tm == 0 and seq % tm == 0 and qm_col % COL_TILE == 0 and A_W == 2 * COL_TILE and MEM_W == COL_TILE
    tiles_per_batch = seq // tm
    qb = qm_col // COL_TILE

    def proj_spec(blk):
        return pl.BlockSpec((tm, COL_TILE), lambda i: (i, blk))

    return pl.pallas_call(
        _outproj_kernel,
        grid=(m // tm,),
        in_specs=[
            pl.BlockSpec((tm, d), lambda i: (i, 0)),
            pl.BlockSpec((tm, A_W), lambda i: (i, 0)),
            proj_spec(qb), proj_spec(qb + 1), proj_spec(qb + 2), proj_spec(qb + 3),
            pl.BlockSpec((N_MEM_TOK, MEM_W), lambda i: (i // tiles_per_batch, 0)),
            pl.BlockSpec((N_MEM_TOK, MEM_W), lambda i: (i // tiles_per_batch, 1)),
            pl.BlockSpec((BRANCH_W, d), lambda i: (0, 0)),
        ],
        out_specs=pl.BlockSpec((tm, d), lambda i: (i, 0)),
        out_shape=jax.ShapeDtypeStruct((m, d), F32),
        compiler_params=_params(1),
        name="outproj",
    )(x2, o_mix2, proj2, proj2, proj2, proj2, memkv_l, memkv_l, w_out)


def _rope_angles(pos, dim, theta):
    inv = theta ** (-jnp.arange(0, dim, 2, dtype=F32) / dim)
    return pos.astype(F32)[:, None] * inv[None, :]


def _head_perm_a():
    r = ROT_DIM_A // 2
    half = HEAD_DIM // 2
    lanes = list(range(0, r)) + list(range(2 * r, half + r)) + list(range(r, 2 * r)) + list(range(half + r, HEAD_DIM))
    assert sorted(lanes) == list(range(HEAD_DIM))
    return jnp.array(lanes, dtype=jnp.int32)


def _tables_a(seq):
    r = ROT_DIM_A // 2
    half = HEAD_DIM // 2
    ang = _rope_angles(jnp.arange(seq, dtype=jnp.int32), ROT_DIM_A, ROPE_THETA)
    cos, sin = jnp.cos(ang), jnp.sin(ang)
    one = jnp.ones((seq, half - r), F32)
    zero = jnp.zeros((seq, half - r), F32)
    return (jnp.concatenate([cos, one, cos, one], axis=-1), jnp.concatenate([-sin, zero, sin, zero], axis=-1))


def _tables_b(seq):
    rows = seq // GRID_W
    row = jnp.repeat(jnp.arange(rows, dtype=jnp.int32), GRID_W)
    col = jnp.tile(jnp.arange(GRID_W, dtype=jnp.int32), rows)
    ang = jnp.concatenate([_rope_angles(row, HEAD_DIM // 2, AXIAL_THETA),
                           _rope_angles(col, HEAD_DIM // 2, AXIAL_THETA)], axis=-1)
    cos, sin = jnp.cos(ang), jnp.sin(ang)
    return (jnp.concatenate([cos, cos], axis=-1), jnp.concatenate([-sin, sin], axis=-1))


def kernel(x, mem, norm_g, mem_norm_g, w_mem_kv, mem_qn_g, mem_kn_g, w_out, w_in_a, qn_a, kn_a, w_in_b, qn_b, kn_b):
    bsz, seq, d = x.shape
    depth = norm_g.shape[0]
    m = bsz * seq
    ones = jnp.ones((HEAD_DIM,), F32)
    tm = min(seq, 1024)
    gpg = A_W // COL_TILE
    ng = len(A_GROUPS)

    memkv = _memkv(mem.reshape(bsz * N_MEM_TOK, d), mem_norm_g, w_mem_kv.astype(BF16), mem_kn_g)
    tab_a = _tables_a(seq)
    tab_b = _tables_b(seq)
    perm_a = _head_perm_a()

    x2 = x.reshape(m, d)
    for i in range(depth):
        j = i // 2
        h2 = _rmsnorm(x2, norm_g[i].reshape(1, d), tm)
        mem_q_gain = mem_qn_g[i] * Q_SCALE
        if i % 2 == 0:
            w = w_in_a[j]
            n_qk = 2 * ng * A_W
            n_qkv = 3 * ng * A_W
            w_qk = w[:, :n_qk].reshape(d, n_qk // HEAD_DIM, HEAD_DIM)[:, :, perm_a].reshape(d, n_qk).astype(BF16)
            qk_gains = [qn_a[j, g][perm_a] * Q_SCALE for g in range(ng) for _ in range(gpg)]
            qk_gains += [kn_a[j, g][perm_a] for g in range(ng) for _ in range(gpg)]
            qk = _proj(h2, w_qk, _granule_gains(qk_gains), tab_a, ("rope",) * 2, seq, tm, "proj_a_qk")
            v = _proj(h2, w[:, n_qk:n_qkv].astype(BF16), _granule_gains([ones] * (ng * gpg)), tab_a,
                      ("plain",) * 2, seq, tm, "proj_a_v")
            mg = _proj(h2, w[:, n_qkv:].astype(BF16), _granule_gains([mem_q_gain, ones, ones, ones]), tab_a,
                       ("norm", "silu", "silu", "silu"), seq, tm, "proj_a_mg")
            o_mix = _dilated(qk.reshape(bsz, seq, n_qk), v.reshape(bsz, seq, n_qkv - n_qk))
            gate_src, qm_col = mg, 0
        else:
            w = w_in_b[j].astype(BF16)
            assert 2 * B_KV_W == COL_TILE
            nq = B_W // COL_TILE
            kinds = ("rope",) * nq + ("rope_then_plain", "norm") + ("silu",) * (BRANCH_W // COL_TILE)
            gains = [qn_b[j] * Q_SCALE] * nq + [kn_b[j], mem_q_gain] + [ones] * (BRANCH_W // COL_TILE)
            proj = _proj(h2, w, _granule_gains(gains), tab_b, kinds, seq, tm, "proj_b")
            o_mix = _gqa(proj.reshape(bsz, seq, w.shape[1]), tq=min(seq, 256), tk=min(seq // 2, 512))
            gate_src, qm_col = proj, B_W + 2 * B_KV_W
        x2 = _outproj(x2, o_mix.reshape(m, A_W), gate_src, memkv[i], w_out[i].astype(BF16), qm_col, seq,
                      min(seq, 512))
    return x2.reshape(bsz, seq, d)
```

```python
import functools
import math

import jax
import jax.numpy as jnp
from jax import lax
from jax.experimental import pallas as pl
from jax.experimental.pallas import tpu as pltpu

F32 = jnp.float32
BF16 = jnp.bfloat16

HEAD_DIM = 128
N_MEM_TOK = 256
MEM_HEADS = 4
MEM_W = MEM_HEADS * HEAD_DIM
A_GROUPS = ((128, 1), (512, 4), (2048, 16))
A_HEADS = 8
A_W = A_HEADS * HEAD_DIM
A_R = 64
B_Q_HEADS = 8
B_KV_HEADS = 2
B_GROUP = B_Q_HEADS // B_KV_HEADS
B_W = B_Q_HEADS * HEAD_DIM
B_KV_W = B_KV_HEADS * HEAD_DIM
BRANCH_W = A_W + MEM_W
ROPE_THETA = 500000.0
ROT_DIM_A = HEAD_DIM // 4
AXIAL_THETA = 10000.0
GRID_W = 64
EPS = 1e-6
NEG = -1e30
Q_SCALE = HEAD_DIM ** -0.5 * math.log2(math.e)

COL_TILE = 512
VMEM_LIMIT = 56 * 1024 * 1024
DIL_CHUNK = 256
DIL_UNROLL = 16


def _rms(x):
    return x * lax.rsqrt(jnp.mean(x * x, axis=-1, keepdims=True) + EPS)


def _nt_dot(a, b):
    return lax.dot_general(a, b, (((1,), (1,)), ((), ())), preferred_element_type=F32)


def _params(n_axes):
    return pltpu.CompilerParams(dimension_semantics=("arbitrary",) * n_axes, vmem_limit_bytes=VMEM_LIMIT)


def _rmsnorm_kernel(x_ref, g_ref, o_ref):
    o_ref[...] = (_rms(x_ref[...]) * g_ref[...]).astype(BF16)


def _rmsnorm(x2, g_row, tm):
    m, d = x2.shape
    return pl.pallas_call(
        _rmsnorm_kernel,
        grid=(m // tm,),
        in_specs=[pl.BlockSpec((tm, d), lambda i: (i, 0)), pl.BlockSpec((1, d), lambda i: (0, 0))],
        out_specs=pl.BlockSpec((tm, d), lambda i: (i, 0)),
        out_shape=jax.ShapeDtypeStruct((m, d), BF16),
        compiler_params=_params(1),
        name="rmsnorm",
    )(x2, g_row)


def _proj_kernel(h_ref, w_ref, gain_ref, cos_ref, sin_ref, ones_ref, o_ref, *, kinds):
    per = COL_TILE // HEAD_DIM
    half = HEAD_DIM // 2
    for t, kind in enumerate(kinds):
        acc = jnp.dot(h_ref[...], w_ref[:, t * COL_TILE:(t + 1) * COL_TILE], preferred_element_type=F32)
        gain = gain_ref[t, 0:1, :]
        if kind in ("rope", "rope_then_plain"):
            g_cos = cos_ref[...] * gain
            g_sin = sin_ref[...] * gain_ref[t, 1:2, :]
            sq = (acc * acc).astype(BF16)
            sumsq = jnp.concatenate(
                [jnp.dot(sq[:, c * HEAD_DIM:(c + 1) * HEAD_DIM], ones_ref[0:HEAD_DIM, 0:HEAD_DIM],
                         preferred_element_type=F32) for c in range(per)], axis=1)
            rstd_all = lax.rsqrt(sumsq * (1.0 / HEAD_DIM) + EPS)
        for c in range(per):
            a = acc[:, c * HEAD_DIM:(c + 1) * HEAD_DIM]
            if kind == "rope" or (kind == "rope_then_plain" and c < per // 2):
                rstd = rstd_all[:, c * HEAD_DIM:(c + 1) * HEAD_DIM]
                a = (a * g_cos + pltpu.roll(a, half, 1) * g_sin) * rstd
            elif kind == "norm":
                a = _rms(a) * gain
            elif kind == "silu":
                a = a / (1.0 + jnp.exp(-a))
            lo = t * COL_TILE + c * HEAD_DIM
            o_ref[:, lo:lo + HEAD_DIM] = a.astype(BF16)


def _proj(h2, w, gains, tables, kinds, seq, tm, name):
    m, d = h2.shape
    n = w.shape[1]
    tn = len(kinds) * COL_TILE
    assert m % tm == 0 and seq % tm == 0 and n % tn == 0
    pos_blocks = seq // tm
    tab_spec = pl.BlockSpec((tm, HEAD_DIM), lambda i, j: (i % pos_blocks, 0))
    lane_head = jnp.arange(COL_TILE, dtype=jnp.int32) // HEAD_DIM
    head_ones = (lane_head[:, None] == lane_head[None, :]).astype(BF16)
    return pl.pallas_call(
        functools.partial(_proj_kernel, kinds=kinds),
        grid=(m // tm, n // tn),
        in_specs=[
            pl.BlockSpec((tm, d), lambda i, j: (i, 0)),
            pl.BlockSpec((d, tn), lambda i, j: (0, j)),
            pl.BlockSpec((len(kinds), 2, HEAD_DIM), lambda i, j: (j, 0, 0)),
            tab_spec, tab_spec,
            pl.BlockSpec((COL_TILE, COL_TILE), lambda i, j: (0, 0)),
        ],
        out_specs=pl.BlockSpec((tm, tn), lambda i, j: (i, j)),
        out_shape=jax.ShapeDtypeStruct((m, n), BF16),
        compiler_params=_params(2),
        name=name,
    )(h2, w, gains, *tables, head_ones)


def _granule_gains(rows):
    g = jnp.stack(rows)
    return jnp.stack([g, jnp.roll(g, HEAD_DIM // 2, axis=-1)], axis=1)


def _memkv_kernel(mem_ref, g_ref, w_ref, kn_ref, o_ref):
    h = (_rms(mem_ref[...]) * g_ref[...]).astype(BF16)
    kv = jnp.dot(h, w_ref[...], preferred_element_type=F32)
    for c in range(MEM_HEADS):
        sl = slice(c * HEAD_DIM, (c + 1) * HEAD_DIM)
        o_ref[:, sl] = (_rms(kv[:, sl]) * kn_ref[...]).astype(BF16)
    o_ref[:, MEM_W:] = kv[:, MEM_W:].astype(BF16)


def _memkv(mem2, mem_norm_g, w_mem_kv, mem_kn_g):
    depth, d = mem_norm_g.shape
    rows = mem2.shape[0]
    return pl.pallas_call(
        _memkv_kernel,
        grid=(depth,),
        in_specs=[
            pl.BlockSpec((rows, d), lambda l: (0, 0)),
            pl.BlockSpec((None, 1, d), lambda l: (l, 0, 0)),
            pl.BlockSpec((None, d, 2 * MEM_W), lambda l: (l, 0, 0)),
            pl.BlockSpec((None, 1, HEAD_DIM), lambda l: (l, 0, 0)),
        ],
        out_specs=pl.BlockSpec((None, rows, 2 * MEM_W), lambda l: (l, 0, 0)),
        out_shape=jax.ShapeDtypeStruct((depth, rows, 2 * MEM_W), BF16),
        compiler_params=_params(1),
        name="memkv",
    )(mem2, mem_norm_g.reshape(depth, 1, d), w_mem_kv, mem_kn_g.reshape(depth, 1, HEAD_DIM))


def _perm_matrix(d):
    n = DIL_CHUNK
    rp = n // d
    row = lax.broadcasted_iota(jnp.int32, (n, n), 0)
    tok = lax.broadcasted_iota(jnp.int32, (n, n), 1)
    src = jnp.bitwise_and(row, rp - 1) * d + lax.shift_right_logical(row, rp.bit_length() - 1)
    return jnp.where(tok == src, 1.0, 0.0).astype(BF16)


def _dilated_kernel(q0, k0, v0, q1, k1, v1, q2, k2, v2, o_ref,
                    qp_ref, kp_ref, vp_ref, oacc_ref, lacc_ref, bias_ref, *, seq):
    r = A_R
    qa = lax.broadcasted_iota(jnp.int32, (r, 3 * r), 0)
    kc = lax.broadcasted_iota(jnp.int32, (r, 3 * r), 1)
    band = jnp.abs(kc - r - qa) <= r
    left = kc >= r
    right = kc < 2 * r
    bias_ref[0] = jnp.where(band, 0.0, NEG)
    bias_ref[1] = jnp.where(band & left, 0.0, NEG)
    bias_ref[2] = jnp.where(band & right, 0.0, NEG)
    bias_ref[3] = jnp.where(band & left & right, 0.0, NEG)
    zeros = jnp.zeros((r, HEAD_DIM), BF16)

    for g, ((_, d), (q_ref, k_ref, v_ref)) in enumerate(zip(A_GROUPS, ((q0, k0, v0), (q1, k1, v1), (q2, k2, v2)))):
        length = seq // d
        nsub = length // r
        seg = length + 2 * r
        for res in range(d):
            for ref in (kp_ref, vp_ref):
                ref[res * seg:res * seg + r, :] = zeros
                ref[res * seg + r + length:(res + 1) * seg, :] = zeros
        if d == 1:
            kp_ref[r:r + seq, :] = k_ref[...]
            vp_ref[r:r + seq, :] = v_ref[...]
            q_src = q_ref
        else:
            perm = _perm_matrix(d)
            rp = DIL_CHUNK // d
            for c in range(seq // DIL_CHUNK):
                rows = slice(c * DIL_CHUNK, (c + 1) * DIL_CHUNK)
                qkv = jnp.concatenate([q_ref[rows, :], k_ref[rows, :], v_ref[rows, :]], axis=1)
                y = jnp.dot(perm, qkv, preferred_element_type=F32).astype(BF16)
                for n, (dst, off, pitch) in enumerate(((qp_ref, 0, length), (kp_ref, r, seg), (vp_ref, r, seg))):
                    for res in range(d):
                        start = res * pitch + off + c * rp
                        dst[start:start + rp, :] = y[res * rp:(res + 1) * rp, n * HEAD_DIM:(n + 1) * HEAD_DIM]
            q_src = qp_ref

        def score_stage(units, d=d, nsub=nsub, q_src=q_src):
            out = []
            for u in units:
                res, s = divmod(u, nsub)
                k_start = u * r + res * (2 * r)
                edge = (1 if s == 0 else 0) + (2 if s == nsub - 1 else 0)
                sc = _nt_dot(q_src[u * r:(u + 1) * r, :], kp_ref[k_start:k_start + 3 * r, :]) + bias_ref[edge]
                out.append((res, s, k_start, sc))
            return out

        def softmax_stage(blocks, g=g, d=d):
            soft = []
            for res, s, k_start, sc in blocks:
                m = jnp.max(sc, axis=-1, keepdims=True)
                p = jnp.exp2(sc - m)
                den = jnp.sum(p, axis=-1, keepdims=True)
                soft.append((res, s, k_start, m, p, den))
            for res, s, k_start, m, p, den in soft:
                o = jnp.dot(p.astype(BF16), vp_ref[k_start:k_start + 3 * r, :], preferred_element_type=F32) / den
                lse = m + jnp.log2(den)
                tok0 = res + d * (s * r)
                rows = slice(tok0, tok0 + r) if d == 1 else pl.ds(tok0, r, stride=d)
                if g > 0:
                    lp = lacc_ref[rows, :]
                    m2 = jnp.maximum(lp, lse)
                    w0 = jnp.exp2(lp - m2)
                    w1 = jnp.exp2(lse - m2)
                    tot = w0 + w1
                    o = (w0 * oacc_ref[rows, :] + w1 * o) / tot
                    lse = m2 + jnp.log2(tot)
                oacc_ref[rows, :] = o
                if g < len(A_GROUPS) - 1:
                    lacc_ref[rows, :] = jnp.broadcast_to(lse, (r, HEAD_DIM))

        n_units = seq // r
        batches = [range(b0, b0 + DIL_UNROLL) for b0 in range(0, n_units, DIL_UNROLL)]
        pending = score_stage(batches[0])
        for b in range(len(batches)):
            following = score_stage(batches[b + 1]) if b + 1 < len(batches) else None
            softmax_stage(pending)
            pending = following

    o_ref[...] = oacc_ref[...].astype(o_ref.dtype)


def _dilated(qk3, v3):
    bsz, seq, _ = qk3.shape
    ng = len(A_GROUPS)
    d_max = max(d for _, d in A_GROUPS)
    assert seq % (d_max * A_R) == 0 and (seq // A_R) % DIL_UNROLL == 0
    assert seq % DIL_CHUNK == 0
    assert all((seq // d // A_R) & (seq // d // A_R - 1) == 0 for _, d in A_GROUPS)

    def head_spec(blk0):
        return pl.BlockSpec((None, seq, HEAD_DIM), lambda b, h: (b, 0, blk0 + h))

    in_specs, args = [], []
    for g in range(ng):
        in_specs += [head_spec(g * A_HEADS), head_spec((ng + g) * A_HEADS), head_spec(g * A_HEADS)]
        args += [qk3, qk3, v3]
    pad_rows = seq + 2 * A_R * d_max
    return pl.pallas_call(
        functools.partial(_dilated_kernel, seq=seq),
        grid=(bsz, A_HEADS),
        in_specs=in_specs,
        out_specs=pl.BlockSpec((None, seq, HEAD_DIM), lambda b, h: (b, 0, h)),
        out_shape=jax.ShapeDtypeStruct((bsz, seq, A_W), BF16),
        scratch_shapes=[
            pltpu.VMEM((seq, HEAD_DIM), BF16),
            pltpu.VMEM((pad_rows, HEAD_DIM), BF16),
            pltpu.VMEM((pad_rows, HEAD_DIM), BF16),
            pltpu.VMEM((seq, HEAD_DIM), F32),
            pltpu.VMEM((seq, HEAD_DIM), F32),
            pltpu.VMEM((4, A_R, 3 * A_R), F32),
        ],
        compiler_params=_params(2),
        name="dilated",
    )(*args)


def _gqa_kernel(q_ref, k_ref, v_ref, o_ref, qs_ref, va_ref, s_ref, p_ref, a_ref, m_ref, acc_ref, *, tk):
    tq = q_ref.shape[0]
    n = k_ref.shape[0] // tk

    @pl.when(pl.program_id(2) == 0)
    def _():
        va_ref[:, :HEAD_DIM] = v_ref[...]
        va_ref[:, HEAD_DIM:] = jnp.ones((va_ref.shape[0], HEAD_DIM), BF16)

    for h in range(B_GROUP):
        qs_ref[h * tq:(h + 1) * tq, :] = q_ref[:, h * HEAD_DIM:(h + 1) * HEAD_DIM]
    m_ref[...] = jnp.full(m_ref.shape, NEG, F32)
    acc_ref[...] = jnp.zeros(acc_ref.shape, F32)

    def scores(c, slot):
        s_ref[slot] = _nt_dot(qs_ref[...], k_ref[c * tk:(c + 1) * tk, :])

    def softmax(slot):
        blocks = [s_ref[slot, :, j * HEAD_DIM:(j + 1) * HEAD_DIM] for j in range(tk // HEAD_DIM)]
        part = blocks[0]
        for blk in blocks[1:]:
            part = jnp.maximum(part, blk)
        m = m_ref[...]
        m_new = jnp.maximum(m, jnp.max(part, axis=-1, keepdims=True))
        a_ref[slot] = jnp.exp2(m - m_new)
        m_ref[...] = m_new
        for j, blk in enumerate(blocks):
            p_ref[slot, :, j * HEAD_DIM:(j + 1) * HEAD_DIM] = jnp.exp2(blk - m_new).astype(BF16)

    def pv(c, slot):
        upd = jnp.dot(p_ref[slot], va_ref[c * tk:(c + 1) * tk, :], preferred_element_type=F32)
        alpha = a_ref[slot]
        for j in range(2):
            cols = slice(j * HEAD_DIM, (j + 1) * HEAD_DIM)
            acc_ref[:, cols] = alpha * acc_ref[:, cols] + upd[:, cols]

    scores(0, 0)
    scores(1, 1)
    softmax(0)

    for c in range(0, n - 2, 2):
        scores(c + 2, 0)
        softmax(1)
        pv(c, 0)
        scores(c + 3, 1)
        softmax(0)
        pv(c + 1, 1)
    softmax(1)
    pv(n - 2, 0)
    pv(n - 1, 1)
    out = acc_ref[:, :HEAD_DIM] / acc_ref[:, HEAD_DIM:]
    for h in range(B_GROUP):
        o_ref[:, h * HEAD_DIM:(h + 1) * HEAD_DIM] = out[h * tq:(h + 1) * tq, :].astype(o_ref.dtype)


def _gqa(proj3, tq, tk):
    bsz, seq, _ = proj3.shape
    assert seq % tq == 0 and seq % (2 * tk) == 0
    qw = B_GROUP * HEAD_DIM
    k_blk = B_W // HEAD_DIM
    v_blk = (B_W + B_KV_W) // HEAD_DIM
    rows = B_GROUP * tq
    return pl.pallas_call(
        functools.partial(_gqa_kernel, tk=tk),
        grid=(bsz, B_KV_HEADS, seq // tq),
        in_specs=[
            pl.BlockSpec((None, tq, qw), lambda b, kh, i: (b, i, kh)),
            pl.BlockSpec((None, seq, HEAD_DIM), lambda b, kh, i: (b, 0, k_blk + kh)),
            pl.BlockSpec((None, seq, HEAD_DIM), lambda b, kh, i: (b, 0, v_blk + kh)),
        ],
        out_specs=pl.BlockSpec((None, tq, qw), lambda b, kh, i: (b, i, kh)),
        out_shape=jax.ShapeDtypeStruct((bsz, seq, B_W), BF16),
        scratch_shapes=[
            pltpu.VMEM((rows, HEAD_DIM), BF16),
            pltpu.VMEM((seq, 2 * HEAD_DIM), BF16),
            pltpu.VMEM((2, rows, tk), F32),
            pltpu.VMEM((2, rows, tk), BF16),
            pltpu.VMEM((2, rows, HEAD_DIM), F32),
            pltpu.VMEM((rows, HEAD_DIM), F32),
            pltpu.VMEM((rows, 2 * HEAD_DIM), F32),
        ],
        compiler_params=_params(3),
        name="gqa",
    )(proj3, proj3, proj3)


def _outproj_kernel(x_ref, om_ref, qm_ref, g0_ref, g1_ref, g2_ref, mk_ref, mv_ref, w_ref, gn_ref, o_ref, *maybe_h_ref):
    mem_heads = []
    for h in range(MEM_HEADS):
        hs = slice(h * HEAD_DIM, (h + 1) * HEAD_DIM)
        s = _nt_dot(qm_ref[:, hs], mk_ref[:, hs])
        p = jnp.exp2(s - jnp.max(s, axis=-1, keepdims=True))
        den = jnp.sum(p, axis=-1, keepdims=True)
        mem_heads.append(jnp.dot(p.astype(BF16), mv_ref[:, hs], preferred_element_type=F32) / den)
    o_mem = jnp.concatenate(mem_heads, axis=-1)

    half = COL_TILE
    y0 = (om_ref[:, :half].astype(F32) * g0_ref[...].astype(F32)).astype(BF16)
    y1 = (om_ref[:, half:].astype(F32) * g1_ref[...].astype(F32)).astype(BF16)
    y2 = (o_mem * g2_ref[...].astype(F32)).astype(BF16)
    acc = jnp.dot(y0, w_ref[0:half, :], preferred_element_type=F32)
    acc = acc + jnp.dot(y1, w_ref[half:2 * half, :], preferred_element_type=F32)
    acc = acc + jnp.dot(y2, w_ref[2 * half:3 * half, :], preferred_element_type=F32)
    out = x_ref[...] + acc
    o_ref[...] = out
    if maybe_h_ref:
        maybe_h_ref[0][...] = (_rms(out) * gn_ref[...]).astype(BF16)


def _outproj(x2, o_mix2, proj2, memkv_l, w_out, qm_col, seq, tm, next_norm_g):
    m, d = x2.shape
    emit_h = next_norm_g is not None
    gn = (next_norm_g if emit_h else jnp.ones((d,), F32)).reshape(1, d)
    row_spec = pl.BlockSpec((tm, d), lambda i: (i, 0))
    assert m % tm == 0 and seq % tm == 0 and qm_col % COL_TILE == 0 and A_W == 2 * COL_TILE and MEM_W == COL_TILE
    tiles_per_batch = seq // tm
    qb = qm_col // COL_TILE

    def proj_spec(blk):
        return pl.BlockSpec((tm, COL_TILE), lambda i: (i, blk))

    outs = pl.pallas_call(
        _outproj_kernel,
        grid=(m // tm,),
        in_specs=[
            row_spec,
            pl.BlockSpec((tm, A_W), lambda i: (i, 0)),
            proj_spec(qb), proj_spec(qb + 1), proj_spec(qb + 2), proj_spec(qb + 3),
            pl.BlockSpec((N_MEM_TOK, MEM_W), lambda i: (i // tiles_per_batch, 0)),
            pl.BlockSpec((N_MEM_TOK, MEM_W), lambda i: (i // tiles_per_batch, 1)),
            pl.BlockSpec((BRANCH_W, d), lambda i: (0, 0)),
            pl.BlockSpec((1, d), lambda i: (0, 0)),
        ],
        out_specs=[row_spec, row_spec] if emit_h else [row_spec],
        out_shape=[jax.ShapeDtypeStruct((m, d), F32)] + ([jax.ShapeDtypeStruct((m, d), BF16)] if emit_h else []),
        compiler_params=_params(1),
        name="outproj",
    )(x2, o_mix2, proj2, proj2, proj2, proj2, memkv_l, memkv_l, w_out, gn)
    return outs[0], (outs[1] if emit_h else None)


def _rope_angles(pos, dim, theta):
    inv = theta ** (-jnp.arange(0, dim, 2, dtype=F32) / dim)
    return pos.astype(F32)[:, None] * inv[None, :]


def _head_perm_a():
    r = ROT_DIM_A // 2
    half = HEAD_DIM // 2
    lanes = list(range(0, r)) + list(range(2 * r, half + r)) + list(range(r, 2 * r)) + list(range(half + r, HEAD_DIM))
    assert sorted(lanes) == list(range(HEAD_DIM))
    return jnp.array(lanes, dtype=jnp.int32)


def _tables_a(seq):
    r = ROT_DIM_A // 2
    half = HEAD_DIM // 2
    ang = _rope_angles(jnp.arange(seq, dtype=jnp.int32), ROT_DIM_A, ROPE_THETA)
    cos, sin = jnp.cos(ang), jnp.sin(ang)
    one = jnp.ones((seq, half - r), F32)
    zero = jnp.zeros((seq, half - r), F32)
    return (jnp.concatenate([cos, one, cos, one], axis=-1), jnp.concatenate([-sin, zero, sin, zero], axis=-1))


def _tables_b(seq):
    rows = seq // GRID_W
    row = jnp.repeat(jnp.arange(rows, dtype=jnp.int32), GRID_W)
    col = jnp.tile(jnp.arange(GRID_W, dtype=jnp.int32), rows)
    ang = jnp.concatenate([_rope_angles(row, HEAD_DIM // 2, AXIAL_THETA),
                           _rope_angles(col, HEAD_DIM // 2, AXIAL_THETA)], axis=-1)
    cos, sin = jnp.cos(ang), jnp.sin(ang)
    return (jnp.concatenate([cos, cos], axis=-1), jnp.concatenate([-sin, sin], axis=-1))


def kernel(x, mem, norm_g, mem_norm_g, w_mem_kv, mem_qn_g, mem_kn_g, w_out, w_in_a, qn_a, kn_a, w_in_b, qn_b, kn_b):
    bsz, seq, d = x.shape
    depth = norm_g.shape[0]
    m = bsz * seq
    ones = jnp.ones((HEAD_DIM,), F32)
    tm = min(seq, 1024)
    gpg = A_W // COL_TILE
    ng = len(A_GROUPS)

    memkv = _memkv(mem.reshape(bsz * N_MEM_TOK, d), mem_norm_g, w_mem_kv.astype(BF16), mem_kn_g)
    tab_a = _tables_a(seq)
    tab_b = _tables_b(seq)
    perm_a = _head_perm_a()

    x2 = x.reshape(m, d)
    h2 = _rmsnorm(x2, norm_g[0].reshape(1, d), tm)
    for i in range(depth):
        j = i // 2
        mem_q_gain = mem_qn_g[i] * Q_SCALE
        if i % 2 == 0:
            w = w_in_a[j]
            n_qk = 2 * ng * A_W
            n_qkv = 3 * ng * A_W
            w_qk = w[:, :n_qk].reshape(d, n_qk // HEAD_DIM, HEAD_DIM)[:, :, perm_a].reshape(d, n_qk).astype(BF16)
            qk_gains = [qn_a[j, g][perm_a] * Q_SCALE for g in range(ng) for _ in range(gpg)]
            qk_gains += [kn_a[j, g][perm_a] for g in range(ng) for _ in range(gpg)]
            qk = _proj(h2, w_qk, _granule_gains(qk_gains), tab_a, ("rope",) * 2, seq, tm, "proj_a_qk")
            v = _proj(h2, w[:, n_qk:n_qkv].astype(BF16), _granule_gains([ones] * (ng * gpg)), tab_a,
                      ("plain",) * 2, seq, tm, "proj_a_v")
            mg = _proj(h2, w[:, n_qkv:].astype(BF16), _granule_gains([mem_q_gain, ones, ones, ones]), tab_a,
                       ("norm", "silu", "silu", "silu"), seq, tm, "proj_a_mg")
            o_mix = _dilated(qk.reshape(bsz, seq, n_qk), v.reshape(bsz, seq, n_qkv - n_qk))
            gate_src, qm_col = mg, 0
        else:
            w = w_in_b[j].astype(BF16)
            assert 2 * B_KV_W == COL_TILE
            nq = B_W // COL_TILE
            kinds = ("rope",) * nq + ("rope_then_plain", "norm") + ("silu",) * (BRANCH_W // COL_TILE)
            gains = [qn_b[j] * Q_SCALE] * nq + [kn_b[j], mem_q_gain] + [ones] * (BRANCH_W // COL_TILE)
            proj = _proj(h2, w, _granule_gains(gains), tab_b, kinds, seq, tm, "proj_b")
            o_mix = _gqa(proj.reshape(bsz, seq, w.shape[1]), tq=min(seq, 256), tk=min(seq // 2, 512))
            gate_src, qm_col = proj, B_W + 2 * B_KV_W
        x2, h2 = _outproj(x2, o_mix.reshape(m, A_W), gate_src, memkv[i], w_out[i].astype(BF16), qm_col, seq, tm,
                          norm_g[i + 1] if i + 1 < depth else None)
    return x2.reshape(bsz, seq, d)
```

```python
import functools
import math

import jax
import jax.numpy as jnp
from jax import lax
from jax.experimental import pallas as pl
from jax.experimental.pallas import tpu as pltpu

F32 = jnp.float32
BF16 = jnp.bfloat16

HEAD_DIM = 128
N_MEM_TOK = 256
MEM_HEADS = 4
MEM_W = MEM_HEADS * HEAD_DIM
A_GROUPS = ((128, 1), (512, 4), (2048, 16))
A_HEADS = 8
A_W = A_HEADS * HEAD_DIM
A_R = 64
B_Q_HEADS = 8
B_KV_HEADS = 2
B_GROUP = B_Q_HEADS // B_KV_HEADS
B_W = B_Q_HEADS * HEAD_DIM
B_KV_W = B_KV_HEADS * HEAD_DIM
BRANCH_W = A_W + MEM_W
ROPE_THETA = 500000.0
ROT_DIM_A = HEAD_DIM // 4
AXIAL_THETA = 10000.0
GRID_W = 64
EPS = 1e-6
NEG = -1e30
Q_SCALE = HEAD_DIM ** -0.5 * math.log2(math.e)

COL_TILE = 512
VMEM_LIMIT = 56 * 1024 * 1024
DIL_CHUNK = 256
DIL_UNROLL = 16


def _rms(x):
    return x * lax.rsqrt(jnp.mean(x * x, axis=-1, keepdims=True) + EPS)


def _nt_dot(a, b):
    return lax.dot_general(a, b, (((1,), (1,)), ((), ())), preferred_element_type=F32)


def _params(n_axes):
    return pltpu.CompilerParams(dimension_semantics=("arbitrary",) * n_axes, vmem_limit_bytes=VMEM_LIMIT)


def _rmsnorm_kernel(x_ref, g_ref, o_ref):
    o_ref[...] = (_rms(x_ref[...]) * g_ref[...]).astype(BF16)


def _rmsnorm(x2, g_row, tm):
    m, d = x2.shape
    return pl.pallas_call(
        _rmsnorm_kernel,
        grid=(m // tm,),
        in_specs=[pl.BlockSpec((tm, d), lambda i: (i, 0)), pl.BlockSpec((1, d), lambda i: (0, 0))],
        out_specs=pl.BlockSpec((tm, d), lambda i: (i, 0)),
        out_shape=jax.ShapeDtypeStruct((m, d), BF16),
        compiler_params=_params(1),
        name="rmsnorm",
    )(x2, g_row)


def _proj_kernel(h_ref, w_ref, gain_ref, cos_ref, sin_ref, ones_ref, o_ref, *, kinds):
    per = COL_TILE // HEAD_DIM
    half = HEAD_DIM // 2
    for t, kind in enumerate(kinds):
        acc = jnp.dot(h_ref[...], w_ref[:, t * COL_TILE:(t + 1) * COL_TILE], preferred_element_type=F32)
        gain = gain_ref[t, 0:1, :]
        if kind in ("rope", "rope_then_plain"):
            g_cos = cos_ref[...] * gain
            g_sin = sin_ref[...] * gain_ref[t, 1:2, :]
            sq = (acc * acc).astype(BF16)
            sumsq = jnp.concatenate(
                [jnp.dot(sq[:, c * HEAD_DIM:(c + 1) * HEAD_DIM], ones_ref[0:HEAD_DIM, 0:HEAD_DIM],
                         preferred_element_type=F32) for c in range(per)], axis=1)
            rstd_all = lax.rsqrt(sumsq * (1.0 / HEAD_DIM) + EPS)
        for c in range(per):
            a = acc[:, c * HEAD_DIM:(c + 1) * HEAD_DIM]
            if kind == "rope" or (kind == "rope_then_plain" and c < per // 2):
                rstd = rstd_all[:, c * HEAD_DIM:(c + 1) * HEAD_DIM]
                a = (a * g_cos + pltpu.roll(a, half, 1) * g_sin) * rstd
            elif kind == "norm":
                a = _rms(a) * gain
            elif kind == "silu":
                a = a / (1.0 + jnp.exp(-a))
            lo = t * COL_TILE + c * HEAD_DIM
            o_ref[:, lo:lo + HEAD_DIM] = a.astype(BF16)


def _proj(h2, w, gains, tables, kinds, seq, tm, name):
    m, d = h2.shape
    n = w.shape[1]
    tn = len(kinds) * COL_TILE
    assert m % tm == 0 and seq % tm == 0 and n % tn == 0
    pos_blocks = seq // tm
    tab_spec = pl.BlockSpec((tm, HEAD_DIM), lambda i, j: (i % pos_blocks, 0))
    lane_head = jnp.arange(COL_TILE, dtype=jnp.int32) // HEAD_DIM
    head_ones = (lane_head[:, None] == lane_head[None, :]).astype(BF16)
    return pl.pallas_call(
        functools.partial(_proj_kernel, kinds=kinds),
        grid=(m // tm, n // tn),
        in_specs=[
            pl.BlockSpec((tm, d), lambda i, j: (i, 0)),
            pl.BlockSpec((d, tn), lambda i, j: (0, j)),
            pl.BlockSpec((len(kinds), 2, HEAD_DIM), lambda i, j: (j, 0, 0)),
            tab_spec, tab_spec,
            pl.BlockSpec((COL_TILE, COL_TILE), lambda i, j: (0, 0)),
        ],
        out_specs=pl.BlockSpec((tm, tn), lambda i, j: (i, j)),
        out_shape=jax.ShapeDtypeStruct((m, n), BF16),
        compiler_params=_params(2),
        name=name,
    )(h2, w, gains, *tables, head_ones)


def _granule_gains(rows):
    g = jnp.stack(rows)
    return jnp.stack([g, jnp.roll(g, HEAD_DIM // 2, axis=-1)], axis=1)


def _memkv_kernel(mem_ref, g_ref, w_ref, kn_ref, o_ref):
    h = (_rms(mem_ref[...]) * g_ref[...]).astype(BF16)
    kv = jnp.dot(h, w_ref[...], preferred_element_type=F32)
    for c in range(MEM_HEADS):
        sl = slice(c * HEAD_DIM, (c + 1) * HEAD_DIM)
        o_ref[:, sl] = (_rms(kv[:, sl]) * kn_ref[...]).astype(BF16)
    o_ref[:, MEM_W:] = kv[:, MEM_W:].astype(BF16)


def _memkv(mem2, mem_norm_g, w_mem_kv, mem_kn_g):
    depth, d = mem_norm_g.shape
    rows = mem2.shape[0]
    return pl.pallas_call(
        _memkv_kernel,
        grid=(depth,),
        in_specs=[
            pl.BlockSpec((rows, d), lambda l: (0, 0)),
            pl.BlockSpec((None, 1, d), lambda l: (l, 0, 0)),
            pl.BlockSpec((None, d, 2 * MEM_W), lambda l: (l, 0, 0)),
            pl.BlockSpec((None, 1, HEAD_DIM), lambda l: (l, 0, 0)),
        ],
        out_specs=pl.BlockSpec((None, rows, 2 * MEM_W), lambda l: (l, 0, 0)),
        out_shape=jax.ShapeDtypeStruct((depth, rows, 2 * MEM_W), BF16),
        compiler_params=_params(1),
        name="memkv",
    )(mem2, mem_norm_g.reshape(depth, 1, d), w_mem_kv, mem_kn_g.reshape(depth, 1, HEAD_DIM))


def _perm_matrix(d):
    n = DIL_CHUNK
    rp = n // d
    row = lax.broadcasted_iota(jnp.int32, (n, n), 0)
    tok = lax.broadcasted_iota(jnp.int32, (n, n), 1)
    src = jnp.bitwise_and(row, rp - 1) * d + lax.shift_right_logical(row, rp.bit_length() - 1)
    return jnp.where(tok == src, 1.0, 0.0).astype(BF16)


def _dilated_kernel(q0, k0, v0, q1, k1, v1, q2, k2, v2, o_ref,
                    qp_ref, kp_ref, vp_ref, oacc_ref, lacc_ref, bias_ref, *, seq):
    r = A_R
    qa = lax.broadcasted_iota(jnp.int32, (r, 3 * r), 0)
    kc = lax.broadcasted_iota(jnp.int32, (r, 3 * r), 1)
    band = jnp.abs(kc - r - qa) <= r
    left = kc >= r
    right = kc < 2 * r
    bias_ref[0] = jnp.where(band, 0.0, NEG)
    bias_ref[1] = jnp.where(band & left, 0.0, NEG)
    bias_ref[2] = jnp.where(band & right, 0.0, NEG)
    bias_ref[3] = jnp.where(band & left & right, 0.0, NEG)
    zeros = jnp.zeros((r, HEAD_DIM), BF16)

    for g, ((_, d), (q_ref, k_ref, v_ref)) in enumerate(zip(A_GROUPS, ((q0, k0, v0), (q1, k1, v1), (q2, k2, v2)))):
        length = seq // d
        nsub = length // r
        seg = length + 2 * r
        for res in range(d):
            for ref in (kp_ref, vp_ref):
                ref[res * seg:res * seg + r, :] = zeros
                ref[res * seg + r + length:(res + 1) * seg, :] = zeros
        if d == 1:
            kp_ref[r:r + seq, :] = k_ref[...]
            vp_ref[r:r + seq, :] = v_ref[...]
            q_src = q_ref
        else:
            perm = _perm_matrix(d)
            rp = DIL_CHUNK // d
            for c in range(seq // DIL_CHUNK):
                rows = slice(c * DIL_CHUNK, (c + 1) * DIL_CHUNK)
                qkv = jnp.concatenate([q_ref[rows, :], k_ref[rows, :], v_ref[rows, :]], axis=1)
                y = jnp.dot(perm, qkv, preferred_element_type=F32).astype(BF16)
                for n, (dst, off, pitch) in enumerate(((qp_ref, 0, length), (kp_ref, r, seg), (vp_ref, r, seg))):
                    for res in range(d):
                        start = res * pitch + off + c * rp
                        dst[start:start + rp, :] = y[res * rp:(res + 1) * rp, n * HEAD_DIM:(n + 1) * HEAD_DIM]
            q_src = qp_ref

        def score_stage(units, d=d, nsub=nsub, q_src=q_src):
            out = []
            for u in units:
                res, s = divmod(u, nsub)
                k_start = u * r + res * (2 * r)
                edge = (1 if s == 0 else 0) + (2 if s == nsub - 1 else 0)
                sc = _nt_dot(q_src[u * r:(u + 1) * r, :], kp_ref[k_start:k_start + 3 * r, :]) + bias_ref[edge]
                out.append((res, s, k_start, sc))
            return out

        def softmax_stage(blocks, g=g, d=d):
            soft = []
            for res, s, k_start, sc in blocks:
                m = jnp.max(sc, axis=-1, keepdims=True)
                p = jnp.exp2(sc - m)
                den = jnp.sum(p, axis=-1, keepdims=True)
                soft.append((res, s, k_start, m, p, den))
            for res, s, k_start, m, p, den in soft:
                o = jnp.dot(p.astype(BF16), vp_ref[k_start:k_start + 3 * r, :], preferred_element_type=F32) / den
                lse = m + jnp.log2(den)
                tok0 = res + d * (s * r)
                rows = slice(tok0, tok0 + r) if d == 1 else pl.ds(tok0, r, stride=d)
                if g > 0:
                    lp = lacc_ref[rows, :]
                    m2 = jnp.maximum(lp, lse)
                    w0 = jnp.exp2(lp - m2)
                    w1 = jnp.exp2(lse - m2)
                    tot = w0 + w1
                    o = (w0 * oacc_ref[rows, :] + w1 * o) / tot
                    lse = m2 + jnp.log2(tot)
                oacc_ref[rows, :] = o
                if g < len(A_GROUPS) - 1:
                    lacc_ref[rows, :] = jnp.broadcast_to(lse, (r, HEAD_DIM))

        n_units = seq // r
        batches = [range(b0, b0 + DIL_UNROLL) for b0 in range(0, n_units, DIL_UNROLL)]
        pending = score_stage(batches[0])
        for b in range(len(batches)):
            following = score_stage(batches[b + 1]) if b + 1 < len(batches) else None
            softmax_stage(pending)
            pending = following

    o_ref[...] = oacc_ref[...].astype(o_ref.dtype)


def _dilated(qk3, v3):
    bsz, seq, _ = qk3.shape
    ng = len(A_GROUPS)
    d_max = max(d for _, d in A_GROUPS)
    assert seq % (d_max * A_R) == 0 and (seq // A_R) % DIL_UNROLL == 0
    assert seq % DIL_CHUNK == 0
    assert all((seq // d // A_R) & (seq // d // A_R - 1) == 0 for _, d in A_GROUPS)

    def head_spec(blk0):
        return pl.BlockSpec((None, seq, HEAD_DIM), lambda b, h: (b, 0, blk0 + h))

    in_specs, args = [], []
    for g in range(ng):
        in_specs += [head_spec(g * A_HEADS), head_spec((ng + g) * A_HEADS), head_spec(g * A_HEADS)]
        args += [qk3, qk3, v3]
    pad_rows = seq + 2 * A_R * d_max
    return pl.pallas_call(
        functools.partial(_dilated_kernel, seq=seq),
        grid=(bsz, A_HEADS),
        in_specs=in_specs,
        out_specs=pl.BlockSpec((None, seq, HEAD_DIM), lambda b, h: (b, 0, h)),
        out_shape=jax.ShapeDtypeStruct((bsz, seq, A_W), BF16),
        scratch_shapes=[
            pltpu.VMEM((seq, HEAD_DIM), BF16),
            pltpu.VMEM((pad_rows, HEAD_DIM), BF16),
            pltpu.VMEM((pad_rows, HEAD_DIM), BF16),
            pltpu.VMEM((seq, HEAD_DIM), F32),
            pltpu.VMEM((seq, HEAD_DIM), F32),
            pltpu.VMEM((4, A_R, 3 * A_R), F32),
        ],
        compiler_params=_params(2),
        name="dilated",
    )(*args)


def _gqa_kernel(q_ref, k_ref, v_ref, o_ref, qs_ref, va_ref, s_ref, p_ref, a_ref, m_ref, acc_ref, *, tk):
    n_sub = qs_ref.shape[0]
    tq = q_ref.shape[0] // n_sub
    n = k_ref.shape[0] // tk

    @pl.when(pl.program_id(2) == 0)
    def _():
        va_ref[:, :HEAD_DIM] = v_ref[...]
        va_ref[:, HEAD_DIM:] = jnp.ones((va_ref.shape[0], HEAD_DIM), BF16)

    def scores(t, c, slot):
        s_ref[t, slot] = _nt_dot(qs_ref[t], k_ref[c * tk:(c + 1) * tk, :])

    def softmax(t, slot):
        blocks = [s_ref[t, slot, :, j * HEAD_DIM:(j + 1) * HEAD_DIM] for j in range(tk // HEAD_DIM)]
        part = blocks[0]
        for blk in blocks[1:]:
            part = jnp.maximum(part, blk)
        m = m_ref[t]
        m_new = jnp.maximum(m, jnp.max(part, axis=-1, keepdims=True))
        a_ref[t, slot] = jnp.exp2(m - m_new)
        m_ref[t] = m_new
        for j, blk in enumerate(blocks):
            p_ref[t, slot, :, j * HEAD_DIM:(j + 1) * HEAD_DIM] = jnp.exp2(blk - m_new).astype(BF16)

    def pv(t, c, slot):
        upd = jnp.dot(p_ref[t, slot], va_ref[c * tk:(c + 1) * tk, :], preferred_element_type=F32)
        alpha = a_ref[t, slot]
        for j in range(2):
            cols = slice(j * HEAD_DIM, (j + 1) * HEAD_DIM)
            acc_ref[t, :, cols] = alpha * acc_ref[t, :, cols] + upd[:, cols]

    def fill(t):
        for h in range(B_GROUP):
            qs_ref[t, h * tq:(h + 1) * tq, :] = q_ref[t * tq:(t + 1) * tq, h * HEAD_DIM:(h + 1) * HEAD_DIM]
        m_ref[t] = jnp.full(m_ref.shape[1:], NEG, F32)
        acc_ref[t] = jnp.zeros(acc_ref.shape[1:], F32)
        scores(t, 0, 0)
        scores(t, 1, 1)
        softmax(t, 0)

    def steady(t):
        for c in range(0, n - 2, 2):
            scores(t, c + 2, 0)
            softmax(t, 1)
            pv(t, c, 0)
            scores(t, c + 3, 1)
            softmax(t, 0)
            pv(t, c + 1, 1)

    def drain(t):
        softmax(t, 1)
        pv(t, n - 2, 0)
        pv(t, n - 1, 1)
        out = acc_ref[t, :, :HEAD_DIM] / acc_ref[t, :, HEAD_DIM:]
        for h in range(B_GROUP):
            o_ref[t * tq:(t + 1) * tq, h * HEAD_DIM:(h + 1) * HEAD_DIM] = out[h * tq:(h + 1) * tq, :].astype(o_ref.dtype)

    fill(0)
    for t in range(n_sub):
        steady(t)
        if t + 1 < n_sub:
            fill(t + 1)
        drain(t)


def _gqa(proj3, tq, tk, n_sub):
    bsz, seq, _ = proj3.shape
    step_q = tq * n_sub
    assert seq % step_q == 0 and seq % (2 * tk) == 0
    qw = B_GROUP * HEAD_DIM
    k_blk = B_W // HEAD_DIM
    v_blk = (B_W + B_KV_W) // HEAD_DIM
    rows = B_GROUP * tq
    return pl.pallas_call(
        functools.partial(_gqa_kernel, tk=tk),
        grid=(bsz, B_KV_HEADS, seq // step_q),
        in_specs=[
            pl.BlockSpec((None, step_q, qw), lambda b, kh, i: (b, i, kh)),
            pl.BlockSpec((None, seq, HEAD_DIM), lambda b, kh, i: (b, 0, k_blk + kh)),
            pl.BlockSpec((None, seq, HEAD_DIM), lambda b, kh, i: (b, 0, v_blk + kh)),
        ],
        out_specs=pl.BlockSpec((None, step_q, qw), lambda b, kh, i: (b, i, kh)),
        out_shape=jax.ShapeDtypeStruct((bsz, seq, B_W), BF16),
        scratch_shapes=[
            pltpu.VMEM((n_sub, rows, HEAD_DIM), BF16),
            pltpu.VMEM((seq, 2 * HEAD_DIM), BF16),
            pltpu.VMEM((n_sub, 2, rows, tk), F32),
            pltpu.VMEM((n_sub, 2, rows, tk), BF16),
            pltpu.VMEM((n_sub, 2, rows, HEAD_DIM), F32),
            pltpu.VMEM((n_sub, rows, HEAD_DIM), F32),
            pltpu.VMEM((n_sub, rows, 2 * HEAD_DIM), F32),
        ],
        compiler_params=_params(3),
        name="gqa",
    )(proj3, proj3, proj3)


def _outproj_kernel(x_ref, om_ref, qm_ref, g0_ref, g1_ref, g2_ref, mk_ref, mv_ref, w_ref, gn_ref, o_ref, *maybe_h_ref):
    mem_heads = []
    for h in range(MEM_HEADS):
        hs = slice(h * HEAD_DIM, (h + 1) * HEAD_DIM)
        s = _nt_dot(qm_ref[:, hs], mk_ref[:, hs])
        p = jnp.exp2(s - jnp.max(s, axis=-1, keepdims=True))
        den = jnp.sum(p, axis=-1, keepdims=True)
        mem_heads.append(jnp.dot(p.astype(BF16), mv_ref[:, hs], preferred_element_type=F32) / den)
    o_mem = jnp.concatenate(mem_heads, axis=-1)

    half = COL_TILE
    y0 = (om_ref[:, :half].astype(F32) * g0_ref[...].astype(F32)).astype(BF16)
    y1 = (om_ref[:, half:].astype(F32) * g1_ref[...].astype(F32)).astype(BF16)
    y2 = (o_mem * g2_ref[...].astype(F32)).astype(BF16)
    acc = jnp.dot(y0, w_ref[0:half, :], preferred_element_type=F32)
    acc = acc + jnp.dot(y1, w_ref[half:2 * half, :], preferred_element_type=F32)
    acc = acc + jnp.dot(y2, w_ref[2 * half:3 * half, :], preferred_element_type=F32)
    out = x_ref[...] + acc
    o_ref[...] = out
    if maybe_h_ref:
        maybe_h_ref[0][...] = (_rms(out) * gn_ref[...]).astype(BF16)


def _outproj(x2, o_mix2, proj2, memkv_l, w_out, qm_col, seq, tm, next_norm_g):
    m, d = x2.shape
    emit_h = next_norm_g is not None
    gn = (next_norm_g if emit_h else jnp.ones((d,), F32)).reshape(1, d)
    row_spec = pl.BlockSpec((tm, d), lambda i: (i, 0))
    assert m % tm == 0 and seq % tm == 0 and qm_col % COL_TILE == 0 and A_W == 2 * COL_TILE and MEM_W == COL_TILE
    tiles_per_batch = seq // tm
    qb = qm_col // COL_TILE

    def proj_spec(blk):
        return pl.BlockSpec((tm, COL_TILE), lambda i: (i, blk))

    outs = pl.pallas_call(
        _outproj_kernel,
        grid=(m // tm,),
        in_specs=[
            row_spec,
            pl.BlockSpec((tm, A_W), lambda i: (i, 0)),
            proj_spec(qb), proj_spec(qb + 1), proj_spec(qb + 2), proj_spec(qb + 3),
            pl.BlockSpec((N_MEM_TOK, MEM_W), lambda i: (i // tiles_per_batch, 0)),
            pl.BlockSpec((N_MEM_TOK, MEM_W), lambda i: (i // tiles_per_batch, 1)),
            pl.BlockSpec((BRANCH_W, d), lambda i: (0, 0)),
            pl.BlockSpec((1, d), lambda i: (0, 0)),
        ],
        out_specs=[row_spec, row_spec] if emit_h else [row_spec],
        out_shape=[jax.ShapeDtypeStruct((m, d), F32)] + ([jax.ShapeDtypeStruct((m, d), BF16)] if emit_h else []),
        compiler_params=_params(1),
        name="outproj",
    )(x2, o_mix2, proj2, proj2, proj2, proj2, memkv_l, memkv_l, w_out, gn)
    return outs[0], (outs[1] if emit_h else None)


def _rope_angles(pos, dim, theta):
    inv = theta ** (-jnp.arange(0, dim, 2, dtype=F32) / dim)
    return pos.astype(F32)[:, None] * inv[None, :]


def _head_perm_a():
    r = ROT_DIM_A // 2
    half = HEAD_DIM // 2
    lanes = list(range(0, r)) + list(range(2 * r, half + r)) + list(range(r, 2 * r)) + list(range(half + r, HEAD_DIM))
    assert sorted(lanes) == list(range(HEAD_DIM))
    return jnp.array(lanes, dtype=jnp.int32)


def _tables_a(seq):
    r = ROT_DIM_A // 2
    half = HEAD_DIM // 2
    ang = _rope_angles(jnp.arange(seq, dtype=jnp.int32), ROT_DIM_A, ROPE_THETA)
    cos, sin = jnp.cos(ang), jnp.sin(ang)
    one = jnp.ones((seq, half - r), F32)
    zero = jnp.zeros((seq, half - r), F32)
    return (jnp.concatenate([cos, one, cos, one], axis=-1), jnp.concatenate([-sin, zero, sin, zero], axis=-1))


def _tables_b(seq):
    rows = seq // GRID_W
    row = jnp.repeat(jnp.arange(rows, dtype=jnp.int32), GRID_W)
    col = jnp.tile(jnp.arange(GRID_W, dtype=jnp.int32), rows)
    ang = jnp.concatenate([_rope_angles(row, HEAD_DIM // 2, AXIAL_THETA),
                           _rope_angles(col, HEAD_DIM // 2, AXIAL_THETA)], axis=-1)
    cos, sin = jnp.cos(ang), jnp.sin(ang)
    return (jnp.concatenate([cos, cos], axis=-1), jnp.concatenate([-sin, sin], axis=-1))


def kernel(x, mem, norm_g, mem_norm_g, w_mem_kv, mem_qn_g, mem_kn_g, w_out, w_in_a, qn_a, kn_a, w_in_b, qn_b, kn_b):
    bsz, seq, d = x.shape
    depth = norm_g.shape[0]
    m = bsz * seq
    ones = jnp.ones((HEAD_DIM,), F32)
    tm = min(seq, 1024)
    gpg = A_W // COL_TILE
    ng = len(A_GROUPS)

    memkv = _memkv(mem.reshape(bsz * N_MEM_TOK, d), mem_norm_g, w_mem_kv.astype(BF16), mem_kn_g)
    tab_a = _tables_a(seq)
    tab_b = _tables_b(seq)
    perm_a = _head_perm_a()

    x2 = x.reshape(m, d)
    h2 = _rmsnorm(x2, norm_g[0].reshape(1, d), tm)
    for i in range(depth):
        j = i // 2
        mem_q_gain = mem_qn_g[i] * Q_SCALE
        if i % 2 == 0:
            w = w_in_a[j]
            n_qk = 2 * ng * A_W
            n_qkv = 3 * ng * A_W
            w_qk = w[:, :n_qk].reshape(d, n_qk // HEAD_DIM, HEAD_DIM)[:, :, perm_a].reshape(d, n_qk).astype(BF16)
            qk_gains = [qn_a[j, g][perm_a] * Q_SCALE for g in range(ng) for _ in range(gpg)]
            qk_gains += [kn_a[j, g][perm_a] for g in range(ng) for _ in range(gpg)]
            qk = _proj(h2, w_qk, _granule_gains(qk_gains), tab_a, ("rope",) * 2, seq, min(seq, 2048), "proj_a_qk")
            v = _proj(h2, w[:, n_qk:n_qkv].astype(BF16), _granule_gains([ones] * (ng * gpg)), tab_a,
                      ("plain",) * 2, seq, min(seq, 2048), "proj_a_v")
            mg = _proj(h2, w[:, n_qkv:].astype(BF16), _granule_gains([mem_q_gain, ones, ones, ones]), tab_a,
                       ("norm", "silu", "silu", "silu"), seq, tm, "proj_a_mg")
            o_mix = _dilated(qk.reshape(bsz, seq, n_qk), v.reshape(bsz, seq, n_qkv - n_qk))
            gate_src, qm_col = mg, 0
        else:
            w = w_in_b[j].astype(BF16)
            assert 2 * B_KV_W == COL_TILE
            nq = B_W // COL_TILE
            kinds = ("rope",) * nq + ("rope_then_plain", "norm") + ("silu",) * (BRANCH_W // COL_TILE)
            gains = [qn_b[j] * Q_SCALE] * nq + [kn_b[j], mem_q_gain] + [ones] * (BRANCH_W // COL_TILE)
            proj = _proj(h2, w, _granule_gains(gains), tab_b, kinds, seq, tm, "proj_b")
            o_mix = _gqa(proj.reshape(bsz, seq, w.shape[1]), tq=min(seq // 2, 256), tk=min(seq // 2, 512), n_sub=2)
            gate_src, qm_col = proj, B_W + 2 * B_KV_W
        x2, h2 = _outproj(x2, o_mix.reshape(m, A_W), gate_src, memkv[i], w_out[i].astype(BF16), qm_col, seq, tm,
                          norm_g[i + 1] if i + 1 < depth else None)
    return x2.reshape(bsz, seq, d)
```

```python
import functools
import math

import jax
import jax.numpy as jnp
from jax import lax
from jax.experimental import pallas as pl
from jax.experimental.pallas import tpu as pltpu

F32 = jnp.float32
BF16 = jnp.bfloat16

HEAD_DIM = 128
N_MEM_TOK = 256
MEM_HEADS = 4
MEM_W = MEM_HEADS * HEAD_DIM
A_GROUPS = ((128, 1), (512, 4), (2048, 16))
A_HEADS = 8
A_W = A_HEADS * HEAD_DIM
A_R = 64
B_Q_HEADS = 8
B_KV_HEADS = 2
B_GROUP = B_Q_HEADS // B_KV_HEADS
B_W = B_Q_HEADS * HEAD_DIM
B_KV_W = B_KV_HEADS * HEAD_DIM
BRANCH_W = A_W + MEM_W
ROPE_THETA = 500000.0
ROT_DIM_A = HEAD_DIM // 4
AXIAL_THETA = 10000.0
GRID_W = 64
EPS = 1e-6
NEG = -1e30
Q_SCALE = HEAD_DIM ** -0.5 * math.log2(math.e)

COL_TILE = 512
VMEM_LIMIT = 56 * 1024 * 1024
DIL_CHUNK = 256
DIL_UNROLL = 16


def _rms(x):
    return x * lax.rsqrt(jnp.mean(x * x, axis=-1, keepdims=True) + EPS)


def _nt_dot(a, b):
    return lax.dot_general(a, b, (((1,), (1,)), ((), ())), preferred_element_type=F32)


def _params(n_axes):
    return pltpu.CompilerParams(dimension_semantics=("arbitrary",) * n_axes, vmem_limit_bytes=VMEM_LIMIT)


def _rmsnorm_kernel(x_ref, g_ref, o_ref):
    o_ref[...] = (_rms(x_ref[...]) * g_ref[...]).astype(BF16)


def _rmsnorm(x2, g_row, tm):
    m, d = x2.shape
    return pl.pallas_call(
        _rmsnorm_kernel,
        grid=(m // tm,),
        in_specs=[pl.BlockSpec((tm, d), lambda i: (i, 0)), pl.BlockSpec((1, d), lambda i: (0, 0))],
        out_specs=pl.BlockSpec((tm, d), lambda i: (i, 0)),
        out_shape=jax.ShapeDtypeStruct((m, d), BF16),
        compiler_params=_params(1),
        name="rmsnorm",
    )(x2, g_row)


def _proj_kernel(h_ref, w_ref, gain_ref, cos_ref, sin_ref, ones_ref, o_ref, *, kinds):
    per = COL_TILE // HEAD_DIM
    half = HEAD_DIM // 2
    for t, kind in enumerate(kinds):
        acc = jnp.dot(h_ref[...], w_ref[:, t * COL_TILE:(t + 1) * COL_TILE], preferred_element_type=F32)
        gain = gain_ref[t, 0:1, :]
        if kind in ("rope", "rope_then_plain"):
            g_cos = cos_ref[...] * gain
            g_sin = sin_ref[...] * gain_ref[t, 1:2, :]
            sq = (acc * acc).astype(BF16)
            sumsq = jnp.concatenate(
                [jnp.dot(sq[:, c * HEAD_DIM:(c + 1) * HEAD_DIM], ones_ref[0:HEAD_DIM, 0:HEAD_DIM],
                         preferred_element_type=F32) for c in range(per)], axis=1)
            rstd_all = lax.rsqrt(sumsq * (1.0 / HEAD_DIM) + EPS)
        for c in range(per):
            a = acc[:, c * HEAD_DIM:(c + 1) * HEAD_DIM]
            if kind == "rope" or (kind == "rope_then_plain" and c < per // 2):
                rstd = rstd_all[:, c * HEAD_DIM:(c + 1) * HEAD_DIM]
                a = (a * g_cos + pltpu.roll(a, half, 1) * g_sin) * rstd
            elif kind == "norm":
                a = _rms(a) * gain
            elif kind == "silu":
                a = a / (1.0 + jnp.exp(-a))
            lo = t * COL_TILE + c * HEAD_DIM
            o_ref[:, lo:lo + HEAD_DIM] = a.astype(BF16)


def _proj(h2, w, gains, tables, kinds, seq, tm, name):
    m, d = h2.shape
    n = w.shape[1]
    tn = len(kinds) * COL_TILE
    assert m % tm == 0 and seq % tm == 0 and n % tn == 0
    pos_blocks = seq // tm
    tab_spec = pl.BlockSpec((tm, HEAD_DIM), lambda i, j: (i % pos_blocks, 0))
    lane_head = jnp.arange(COL_TILE, dtype=jnp.int32) // HEAD_DIM
    head_ones = (lane_head[:, None] == lane_head[None, :]).astype(BF16)
    return pl.pallas_call(
        functools.partial(_proj_kernel, kinds=kinds),
        grid=(m // tm, n // tn),
        in_specs=[
            pl.BlockSpec((tm, d), lambda i, j: (i, 0)),
            pl.BlockSpec((d, tn), lambda i, j: (0, j)),
            pl.BlockSpec((len(kinds), 2, HEAD_DIM), lambda i, j: (j, 0, 0)),
            tab_spec, tab_spec,
            pl.BlockSpec((COL_TILE, COL_TILE), lambda i, j: (0, 0)),
        ],
        out_specs=pl.BlockSpec((tm, tn), lambda i, j: (i, j)),
        out_shape=jax.ShapeDtypeStruct((m, n), BF16),
        compiler_params=_params(2),
        name=name,
    )(h2, w, gains, *tables, head_ones)


def _granule_gains(rows):
    g = jnp.stack(rows)
    return jnp.stack([g, jnp.roll(g, HEAD_DIM // 2, axis=-1)], axis=1)


def _memkv_kernel(mem_ref, g_ref, w_ref, kn_ref, o_ref):
    h = (_rms(mem_ref[...]) * g_ref[...]).astype(BF16)
    kv = jnp.dot(h, w_ref[...], preferred_element_type=F32)
    for c in range(MEM_HEADS):
        sl = slice(c * HEAD_DIM, (c + 1) * HEAD_DIM)
        o_ref[:, sl] = (_rms(kv[:, sl]) * kn_ref[...]).astype(BF16)
    o_ref[:, MEM_W:] = kv[:, MEM_W:].astype(BF16)


def _memkv(mem2, mem_norm_g, w_mem_kv, mem_kn_g):
    depth, d = mem_norm_g.shape
    rows = mem2.shape[0]
    return pl.pallas_call(
        _memkv_kernel,
        grid=(depth,),
        in_specs=[
            pl.BlockSpec((rows, d), lambda l: (0, 0)),
            pl.BlockSpec((None, 1, d), lambda l: (l, 0, 0)),
            pl.BlockSpec((None, d, 2 * MEM_W), lambda l: (l, 0, 0)),
            pl.BlockSpec((None, 1, HEAD_DIM), lambda l: (l, 0, 0)),
        ],
        out_specs=pl.BlockSpec((None, rows, 2 * MEM_W), lambda l: (l, 0, 0)),
        out_shape=jax.ShapeDtypeStruct((depth, rows, 2 * MEM_W), BF16),
        compiler_params=_params(1),
        name="memkv",
    )(mem2, mem_norm_g.reshape(depth, 1, d), w_mem_kv, mem_kn_g.reshape(depth, 1, HEAD_DIM))


def _perm_matrix(d):
    n = DIL_CHUNK
    rp = n // d
    row = lax.broadcasted_iota(jnp.int32, (n, n), 0)
    tok = lax.broadcasted_iota(jnp.int32, (n, n), 1)
    src = jnp.bitwise_and(row, rp - 1) * d + lax.shift_right_logical(row, rp.bit_length() - 1)
    return jnp.where(tok == src, 1.0, 0.0).astype(BF16)


def _dilated_kernel(q0, k0, v0, q1, k1, v1, q2, k2, v2, o_ref,
                    qp_ref, kp_ref, vp_ref, oacc_ref, lacc_ref, bias_ref, *, seq):
    r = A_R
    qa = lax.broadcasted_iota(jnp.int32, (r, 3 * r), 0)
    kc = lax.broadcasted_iota(jnp.int32, (r, 3 * r), 1)
    band = jnp.abs(kc - r - qa) <= r
    left = kc >= r
    right = kc < 2 * r
    bias_ref[0] = jnp.where(band, 0.0, NEG)
    bias_ref[1] = jnp.where(band & left, 0.0, NEG)
    bias_ref[2] = jnp.where(band & right, 0.0, NEG)
    bias_ref[3] = jnp.where(band & left & right, 0.0, NEG)
    zeros = jnp.zeros((r, HEAD_DIM), BF16)

    for g, ((_, d), (q_ref, k_ref, v_ref)) in enumerate(zip(A_GROUPS, ((q0, k0, v0), (q1, k1, v1), (q2, k2, v2)))):
        length = seq // d
        nsub = length // r
        seg = length + 2 * r
        for res in range(d):
            for ref in (kp_ref, vp_ref):
                ref[res * seg:res * seg + r, :] = zeros
                ref[res * seg + r + length:(res + 1) * seg, :] = zeros
        if d == 1:
            kp_ref[r:r + seq, :] = k_ref[...]
            vp_ref[r:r + seq, :] = v_ref[...]
            q_src = q_ref
        else:
            perm = _perm_matrix(d)
            rp = DIL_CHUNK // d
            for c in range(seq // DIL_CHUNK):
                rows = slice(c * DIL_CHUNK, (c + 1) * DIL_CHUNK)
                qkv = jnp.concatenate([q_ref[rows, :], k_ref[rows, :], v_ref[rows, :]], axis=1)
                y = jnp.dot(perm, qkv, preferred_element_type=F32).astype(BF16)
                for n, (dst, off, pitch) in enumerate(((qp_ref, 0, length), (kp_ref, r, seg), (vp_ref, r, seg))):
                    for res in range(d):
                        start = res * pitch + off + c * rp
                        dst[start:start + rp, :] = y[res * rp:(res + 1) * rp, n * HEAD_DIM:(n + 1) * HEAD_DIM]
            q_src = qp_ref

        def score_stage(units, d=d, nsub=nsub, q_src=q_src):
            out = []
            for u in units:
                res, s = divmod(u, nsub)
                k_start = u * r + res * (2 * r)
                edge = (1 if s == 0 else 0) + (2 if s == nsub - 1 else 0)
                sc = _nt_dot(q_src[u * r:(u + 1) * r, :], kp_ref[k_start:k_start + 3 * r, :]) + bias_ref[edge]
                out.append((res, s, k_start, sc))
            return out

        def softmax_stage(blocks, g=g, d=d):
            soft = []
            for res, s, k_start, sc in blocks:
                m = jnp.max(sc, axis=-1, keepdims=True)
                p = jnp.exp2(sc - m)
                den = jnp.sum(p, axis=-1, keepdims=True)
                soft.append((res, s, k_start, m, p, den))
            for res, s, k_start, m, p, den in soft:
                o = jnp.dot(p.astype(BF16), vp_ref[k_start:k_start + 3 * r, :], preferred_element_type=F32) / den
                lse = m + jnp.log2(den)
                tok0 = res + d * (s * r)
                rows = slice(tok0, tok0 + r) if d == 1 else pl.ds(tok0, r, stride=d)
                if g > 0:
                    lp = lacc_ref[rows, :]
                    m2 = jnp.maximum(lp, lse)
                    w0 = jnp.exp2(lp - m2)
                    w1 = jnp.exp2(lse - m2)
                    tot = w0 + w1
                    o = (w0 * oacc_ref[rows, :] + w1 * o) / tot
                    lse = m2 + jnp.log2(tot)
                oacc_ref[rows, :] = o
                if g < len(A_GROUPS) - 1:
                    lacc_ref[rows, :] = jnp.broadcast_to(lse, (r, HEAD_DIM))

        n_units = seq // r
        batches = [range(b0, b0 + DIL_UNROLL) for b0 in range(0, n_units, DIL_UNROLL)]
        pending = score_stage(batches[0])
        for b in range(len(batches)):
            following = score_stage(batches[b + 1]) if b + 1 < len(batches) else None
            softmax_stage(pending)
            pending = following

    o_ref[...] = oacc_ref[...].astype(o_ref.dtype)


def _dilated(qk3, v3):
    bsz, seq, _ = qk3.shape
    ng = len(A_GROUPS)
    d_max = max(d for _, d in A_GROUPS)
    assert seq % (d_max * A_R) == 0 and (seq // A_R) % DIL_UNROLL == 0
    assert seq % DIL_CHUNK == 0
    assert all((seq // d // A_R) & (seq // d // A_R - 1) == 0 for _, d in A_GROUPS)

    def head_spec(blk0):
        return pl.BlockSpec((None, seq, HEAD_DIM), lambda b, h: (b, 0, blk0 + h))

    in_specs, args = [], []
    for g in range(ng):
        in_specs += [head_spec(g * A_HEADS), head_spec((ng + g) * A_HEADS), head_spec(g * A_HEADS)]
        args += [qk3, qk3, v3]
    pad_rows = seq + 2 * A_R * d_max
    return pl.pallas_call(
        functools.partial(_dilated_kernel, seq=seq),
        grid=(bsz, A_HEADS),
        in_specs=in_specs,
        out_specs=pl.BlockSpec((None, seq, HEAD_DIM), lambda b, h: (b, 0, h)),
        out_shape=jax.ShapeDtypeStruct((bsz, seq, A_W), BF16),
        scratch_shapes=[
            pltpu.VMEM((seq, HEAD_DIM), BF16),
            pltpu.VMEM((pad_rows, HEAD_DIM), BF16),
            pltpu.VMEM((pad_rows, HEAD_DIM), BF16),
            pltpu.VMEM((seq, HEAD_DIM), F32),
            pltpu.VMEM((seq, HEAD_DIM), F32),
            pltpu.VMEM((4, A_R, 3 * A_R), F32),
        ],
        compiler_params=_params(2),
        name="dilated",
    )(*args)


def _gqa_kernel(q_ref, k_ref, v_ref, o_ref, qs_ref, va_ref, s_ref, p_ref, a_ref, m_ref, acc_ref, *, tk):
    n_sub = qs_ref.shape[0]
    tq = q_ref.shape[0] // n_sub
    n = k_ref.shape[0] // tk

    @pl.when(pl.program_id(2) == 0)
    def _():
        va_ref[:, :HEAD_DIM] = v_ref[...]
        va_ref[:, HEAD_DIM:] = jnp.ones((va_ref.shape[0], HEAD_DIM), BF16)

    def scores(t, c, slot):
        s_ref[t, slot] = _nt_dot(qs_ref[t], k_ref[c * tk:(c + 1) * tk, :])

    def softmax(t, slot):
        blocks = [s_ref[t, slot, :, j * HEAD_DIM:(j + 1) * HEAD_DIM] for j in range(tk // HEAD_DIM)]
        part = blocks[0]
        for blk in blocks[1:]:
            part = jnp.maximum(part, blk)
        m = m_ref[t]
        m_new = jnp.maximum(m, jnp.max(part, axis=-1, keepdims=True))
        a_ref[t, slot] = jnp.exp2(m - m_new)
        m_ref[t] = m_new
        for j, blk in enumerate(blocks):
            p_ref[t, slot, :, j * HEAD_DIM:(j + 1) * HEAD_DIM] = jnp.exp2(blk - m_new).astype(BF16)

    def pv(t, c, slot):
        upd = jnp.dot(p_ref[t, slot], va_ref[c * tk:(c + 1) * tk, :], preferred_element_type=F32)
        alpha = a_ref[t, slot]
        for j in range(2):
            cols = slice(j * HEAD_DIM, (j + 1) * HEAD_DIM)
            acc_ref[t, :, cols] = alpha * acc_ref[t, :, cols] + upd[:, cols]

    def fill(t):
        for h in range(B_GROUP):
            qs_ref[t, h * tq:(h + 1) * tq, :] = q_ref[t * tq:(t + 1) * tq, h * HEAD_DIM:(h + 1) * HEAD_DIM]
        m_ref[t] = jnp.full(m_ref.shape[1:], NEG, F32)
        acc_ref[t] = jnp.zeros(acc_ref.shape[1:], F32)
        scores(t, 0, 0)
        scores(t, 1, 1)
        softmax(t, 0)

    def steady(t):
        for c in range(0, n - 2, 2):
            scores(t, c + 2, 0)
            softmax(t, 1)
            pv(t, c, 0)
            scores(t, c + 3, 1)
            softmax(t, 0)
            pv(t, c + 1, 1)

    def drain(t):
        softmax(t, 1)
        pv(t, n - 2, 0)
        pv(t, n - 1, 1)
        out = acc_ref[t, :, :HEAD_DIM] / acc_ref[t, :, HEAD_DIM:]
        for h in range(B_GROUP):
            o_ref[t * tq:(t + 1) * tq, h * HEAD_DIM:(h + 1) * HEAD_DIM] = out[h * tq:(h + 1) * tq, :].astype(o_ref.dtype)

    fill(0)
    for t in range(n_sub):
        steady(t)
        if t + 1 < n_sub:
            fill(t + 1)
        drain(t)


def _gqa(proj3, tq, tk, n_sub):
    bsz, seq, _ = proj3.shape
    step_q = tq * n_sub
    assert seq % step_q == 0 and seq % (2 * tk) == 0
    qw = B_GROUP * HEAD_DIM
    k_blk = B_W // HEAD_DIM
    v_blk = (B_W + B_KV_W) // HEAD_DIM
    rows = B_GROUP * tq
    return pl.pallas_call(
        functools.partial(_gqa_kernel, tk=tk),
        grid=(bsz, B_KV_HEADS, seq // step_q),
        in_specs=[
            pl.BlockSpec((None, step_q, qw), lambda b, kh, i: (b, i, kh)),
            pl.BlockSpec((None, seq, HEAD_DIM), lambda b, kh, i: (b, 0, k_blk + kh)),
            pl.BlockSpec((None, seq, HEAD_DIM), lambda b, kh, i: (b, 0, v_blk + kh)),
        ],
        out_specs=pl.BlockSpec((None, step_q, qw), lambda b, kh, i: (b, i, kh)),
        out_shape=jax.ShapeDtypeStruct((bsz, seq, B_W), BF16),
        scratch_shapes=[
            pltpu.VMEM((n_sub, rows, HEAD_DIM), BF16),
            pltpu.VMEM((seq, 2 * HEAD_DIM), BF16),
            pltpu.VMEM((n_sub, 2, rows, tk), F32),
            pltpu.VMEM((n_sub, 2, rows, tk), BF16),
            pltpu.VMEM((n_sub, 2, rows, HEAD_DIM), F32),
            pltpu.VMEM((n_sub, rows, HEAD_DIM), F32),
            pltpu.VMEM((n_sub, rows, 2 * HEAD_DIM), F32),
        ],
        compiler_params=_params(3),
        name="gqa",
    )(proj3, proj3, proj3)


def _outproj_kernel(x_ref, om_ref, qm_ref, g0_ref, g1_ref, g2_ref, mk_ref, mv_ref, w_ref, gn_ref, o_ref, *maybe_h_ref):
    mem_heads = []
    for h in range(MEM_HEADS):
        hs = slice(h * HEAD_DIM, (h + 1) * HEAD_DIM)
        s = _nt_dot(qm_ref[:, hs], mk_ref[:, hs])
        p = jnp.exp2(s - jnp.max(s, axis=-1, keepdims=True))
        den = jnp.sum(p, axis=-1, keepdims=True)
        mem_heads.append(jnp.dot(p.astype(BF16), mv_ref[:, hs], preferred_element_type=F32) / den)
    o_mem = jnp.concatenate(mem_heads, axis=-1)

    half = COL_TILE
    y0 = (om_ref[:, :half].astype(F32) * g0_ref[...].astype(F32)).astype(BF16)
    y1 = (om_ref[:, half:].astype(F32) * g1_ref[...].astype(F32)).astype(BF16)
    y2 = (o_mem * g2_ref[...].astype(F32)).astype(BF16)
    acc = jnp.dot(y0, w_ref[0:half, :], preferred_element_type=F32)
    acc = acc + jnp.dot(y1, w_ref[half:2 * half, :], preferred_element_type=F32)
    acc = acc + jnp.dot(y2, w_ref[2 * half:3 * half, :], preferred_element_type=F32)
    out = x_ref[...] + acc
    o_ref[...] = out
    if maybe_h_ref:
        maybe_h_ref[0][...] = (_rms(out) * gn_ref[...]).astype(BF16)


def _outproj(x2, o_mix2, proj2, memkv_l, w_out, qm_col, seq, tm, next_norm_g):
    m, d = x2.shape
    emit_h = next_norm_g is not None
    gn = (next_norm_g if emit_h else jnp.ones((d,), F32)).reshape(1, d)
    row_spec = pl.BlockSpec((tm, d), lambda i: (i, 0))
    assert m % tm == 0 and seq % tm == 0 and qm_col % COL_TILE == 0 and A_W == 2 * COL_TILE and MEM_W == COL_TILE
    tiles_per_batch = seq // tm
    qb = qm_col // COL_TILE

    def proj_spec(blk):
        return pl.BlockSpec((tm, COL_TILE), lambda i: (i, blk))

    outs = pl.pallas_call(
        _outproj_kernel,
        grid=(m // tm,),
        in_specs=[
            row_spec,
            pl.BlockSpec((tm, A_W), lambda i: (i, 0)),
            proj_spec(qb), proj_spec(qb + 1), proj_spec(qb + 2), proj_spec(qb + 3),
            pl.BlockSpec((N_MEM_TOK, MEM_W), lambda i: (i // tiles_per_batch, 0)),
            pl.BlockSpec((N_MEM_TOK, MEM_W), lambda i: (i // tiles_per_batch, 1)),
            pl.BlockSpec((BRANCH_W, d), lambda i: (0, 0)),
            pl.BlockSpec((1, d), lambda i: (0, 0)),
        ],
        out_specs=[row_spec, row_spec] if emit_h else [row_spec],
        out_shape=[jax.ShapeDtypeStruct((m, d), F32)] + ([jax.ShapeDtypeStruct((m, d), BF16)] if emit_h else []),
        compiler_params=_params(1),
        name="outproj",
    )(x2, o_mix2, proj2, proj2, proj2, proj2, memkv_l, memkv_l, w_out, gn)
    return outs[0], (outs[1] if emit_h else None)


def _rope_angles(pos, dim, theta):
    inv = theta ** (-jnp.arange(0, dim, 2, dtype=F32) / dim)
    return pos.astype(F32)[:, None] * inv[None, :]


def _head_perm_a():
    r = ROT_DIM_A // 2
    half = HEAD_DIM // 2
    lanes = list(range(0, r)) + list(range(2 * r, half + r)) + list(range(r, 2 * r)) + list(range(half + r, HEAD_DIM))
    assert sorted(lanes) == list(range(HEAD_DIM))
    return jnp.array(lanes, dtype=jnp.int32)


def _tables_a(seq):
    r = ROT_DIM_A // 2
    half = HEAD_DIM // 2
    ang = _rope_angles(jnp.arange(seq, dtype=jnp.int32), ROT_DIM_A, ROPE_THETA)
    cos, sin = jnp.cos(ang), jnp.sin(ang)
    one = jnp.ones((seq, half - r), F32)
    zero = jnp.zeros((seq, half - r), F32)
    return (jnp.concatenate([cos, one, cos, one], axis=-1), jnp.concatenate([-sin, zero, sin, zero], axis=-1))


def _tables_b(seq):
    rows = seq // GRID_W
    row = jnp.repeat(jnp.arange(rows, dtype=jnp.int32), GRID_W)
    col = jnp.tile(jnp.arange(GRID_W, dtype=jnp.int32), rows)
    ang = jnp.concatenate([_rope_angles(row, HEAD_DIM // 2, AXIAL_THETA),
                           _rope_angles(col, HEAD_DIM // 2, AXIAL_THETA)], axis=-1)
    cos, sin = jnp.cos(ang), jnp.sin(ang)
    return (jnp.concatenate([cos, cos], axis=-1), jnp.concatenate([-sin, sin], axis=-1))


def kernel(x, mem, norm_g, mem_norm_g, w_mem_kv, mem_qn_g, mem_kn_g, w_out, w_in_a, qn_a, kn_a, w_in_b, qn_b, kn_b):
    bsz, seq, d = x.shape
    depth = norm_g.shape[0]
    m = bsz * seq
    ones = jnp.ones((HEAD_DIM,), F32)
    tm = min(seq, 1024)
    gpg = A_W // COL_TILE
    ng = len(A_GROUPS)

    memkv = _memkv(mem.reshape(bsz * N_MEM_TOK, d), mem_norm_g, w_mem_kv.astype(BF16), mem_kn_g)
    tab_a = _tables_a(seq)
    tab_b = _tables_b(seq)
    perm_a = _head_perm_a()

    x2 = x.reshape(m, d)
    h2 = _rmsnorm(x2, norm_g[0].reshape(1, d), tm)
    for i in range(depth):
        j = i // 2
        mem_q_gain = mem_qn_g[i] * Q_SCALE
        if i % 2 == 0:
            w = w_in_a[j]
            n_qk = 2 * ng * A_W
            n_qkv = 3 * ng * A_W
            w_qk = w[:, :n_qk].reshape(d, n_qk // HEAD_DIM, HEAD_DIM)[:, :, perm_a].reshape(d, n_qk).astype(BF16)
            qk_gains = [qn_a[j, g][perm_a] * Q_SCALE for g in range(ng) for _ in range(gpg)]
            qk_gains += [kn_a[j, g][perm_a] for g in range(ng) for _ in range(gpg)]
            qk = _proj(h2, w_qk, _granule_gains(qk_gains), tab_a, ("rope",) * 2, seq, min(seq, 2048), "proj_a_qk")
            v = _proj(h2, w[:, n_qk:n_qkv].astype(BF16), _granule_gains([ones] * (ng * gpg)), tab_a,
                      ("plain",) * 2, seq, min(seq, 2048), "proj_a_v")
            mg = _proj(h2, w[:, n_qkv:].astype(BF16), _granule_gains([mem_q_gain, ones, ones, ones]), tab_a,
                       ("norm", "silu", "silu", "silu"), seq, tm, "proj_a_mg")
            o_mix = _dilated(qk.reshape(bsz, seq, n_qk), v.reshape(bsz, seq, n_qkv - n_qk))
            gate_src, qm_col = mg, 0
        else:
            w = w_in_b[j].astype(BF16)
            assert 2 * B_KV_W == COL_TILE
            nq = B_W // COL_TILE
            kinds = ("rope",) * nq + ("rope_then_plain", "norm") + ("silu",) * (BRANCH_W // COL_TILE)
            gains = [qn_b[j] * Q_SCALE] * nq + [kn_b[j], mem_q_gain] + [ones] * (BRANCH_W // COL_TILE)
            proj = _proj(h2, w, _granule_gains(gains), tab_b, kinds, seq, tm, "proj_b")
            o_mix = _gqa(proj.reshape(bsz, seq, w.shape[1]), tq=min(seq // 4, 256), tk=min(seq // 2, 512), n_sub=4)
            gate_src, qm_col = proj, B_W + 2 * B_KV_W
        x2, h2 = _outproj(x2, o_mix.reshape(m, A_W), gate_src, memkv[i], w_out[i].astype(BF16), qm_col, seq, tm,
                          norm_g[i + 1] if i + 1 < depth else None)
    return x2.reshape(bsz, seq, d)
```
